```python
import math
import jax
import jax.numpy as jnp
from jax import lax
import numpy as np

D_MODEL = 1024
BATCH = 2
SEQ = 8192
DEPTH = 4
DEC_BATCH = 128
DEC_SEQ = 8
PAST_LEN = 8192
PAGE_SIZE = 128

H_A = 4
DH_A = 128
H_IDX = 8
D_IDX = 64
TOPK_MAX = 256
N_BUCKETS = 32
T5_MAX_DIST = 128
H_B = 4
D_NOPE = 128
D_ROPE = 64
D_V = 128
Q_LORA = 384
KV_LORA = 256
ROPE_THETA = 10000.0
C_A = 2 * DH_A
C_B = KV_LORA + D_ROPE
MIX_WIDTH = H_A * DH_A + H_B * D_V
IN_SPLITS = (H_A * DH_A, DH_A, DH_A, H_IDX * D_IDX, D_IDX, H_IDX, Q_LORA, KV_LORA, D_ROPE)
D_IN = sum(IN_SPLITS)
D_FF = 2816
D_FF_EXP = 2816
N_EXPERTS = 8
TOP_K = 2
N_DENSE = (DEPTH + 1) // 2
N_MOE = DEPTH // 2
ALPHA = (2 * DEPTH) ** 0.25
BETA = (8 * DEPTH) ** -0.25
EPS = 1e-5
Q_BLOCK = 128
MLA_SCALE = (D_NOPE + D_ROPE) ** -0.5

kernel_name = 'hybrid_dsa_mla_deepnorm_step'


def layer_norm(x, g, b):
    xf = x.astype(jnp.float32)
    xc = xf - jnp.mean(xf, axis=-1, keepdims=True)
    y = xc * lax.rsqrt(jnp.mean(xc * xc, axis=-1, keepdims=True) + EPS)
    return (y * g.astype(jnp.float32) + b.astype(jnp.float32)).astype(x.dtype)


def rms_norm(x, g):
    xf = x.astype(jnp.float32)
    y = xf * lax.rsqrt(jnp.mean(xf * xf, axis=-1, keepdims=True) + EPS)
    return (y * g.astype(jnp.float32)).astype(x.dtype)


def rope(x, pos):
    half = D_ROPE // 2
    inv = ROPE_THETA ** (-jnp.arange(half, dtype=jnp.float32) / half)
    ang = pos.astype(jnp.float32)[:, None] * inv[None, :]
    cos = jnp.cos(ang)[None, :, None, :].astype(x.dtype)
    sin = jnp.sin(ang)[None, :, None, :].astype(x.dtype)
    x1, x2 = x[..., :half], x[..., half:]
    return jnp.concatenate([x1 * cos - x2 * sin, x1 * sin + x2 * cos], axis=-1)


def t5_bucket(rel):
    n = jnp.maximum(rel, 0)
    max_exact = N_BUCKETS // 2
    nf = jnp.maximum(n, max_exact).astype(jnp.float32)
    large = max_exact + (jnp.log(nf / max_exact) / math.log(T5_MAX_DIST / max_exact)
                         * (N_BUCKETS - max_exact)).astype(jnp.int32)
    return jnp.where(n < max_exact, n, jnp.minimum(large, N_BUCKETS - 1))


def split_columns(h):
    out, start = [], 0
    for width in IN_SPLITS:
        out.append(h[..., start:start + width])
        start += width
    return out


def mixer_projections(x, pos, w_in_l, idx_ln_g_l, idx_ln_b_l, q_norm_g_l, w_uq_l, kv_norm_g_l, w_uk_l):
    B, L, _ = x.shape
    q_a, k_a, v_a, q_idx, k_idx, w_idx, c_q, c_kv, k_rope = split_columns(x @ w_in_l)
    q_a = q_a.reshape(B, L, H_A, DH_A)
    kv_a = jnp.concatenate([k_a, v_a], axis=-1)
    q_idx = q_idx.reshape(B, L, H_IDX, D_IDX)
    k_idx = layer_norm(k_idx, idx_ln_g_l, idx_ln_b_l)
    w_idx = w_idx * (H_IDX ** -0.5)
    q_b = (rms_norm(c_q, q_norm_g_l) @ w_uq_l).reshape(B, L, H_B, D_NOPE + D_ROPE)
    q_lat = jnp.einsum('blhn,rhn->blhr', q_b[..., :D_NOPE], w_uk_l)
    q_cat = jnp.concatenate([q_lat, rope(q_b[..., D_NOPE:], pos)], axis=-1) * MLA_SCALE
    kv_b = jnp.concatenate([rms_norm(c_kv, kv_norm_g_l),
                            rope(k_rope[:, :, None, :], pos)[:, :, 0, :]], axis=-1)
    return q_a, kv_a, q_idx, k_idx, w_idx, q_cat, kv_b


def dsa_core(q_a, q_idx, w_idx, pos_q, k_idx_keys, gather_rows, n_keep, rel_bias):
    n_keys = k_idx_keys.shape[1]
    s = jnp.einsum('bqhd,bld->bqhl', q_idx, k_idx_keys) * (D_IDX ** -0.5)
    score = jnp.einsum('bqhl,bqh->bql', jax.nn.relu(s), w_idx).astype(jnp.float32)
    visible = jnp.arange(n_keys, dtype=jnp.int32)[None, :] <= pos_q[:, None]
    score = jnp.where(visible[None], score, -jnp.inf)
    _, sel = lax.top_k(score, n_keep)
    rows = gather_rows(sel)
    k_sel, v_sel = rows[..., :DH_A], rows[..., DH_A:]
    logits = jnp.einsum('bqhd,bqkd->bhqk', q_a, k_sel).astype(jnp.float32) * (DH_A ** -0.5)
    rel = pos_q[None, :, None] - sel
    bias = rel_bias.astype(jnp.float32)[t5_bucket(rel)]
    logits = logits + jnp.moveaxis(bias, -1, 1)
    logits = jnp.where((rel >= 0)[:, None], logits, -jnp.inf)
    p = jax.nn.softmax(logits, axis=-1).astype(v_sel.dtype)
    return jnp.einsum('bhqk,bqkd->bqhd', p, v_sel)


def dsa_prompt(q_a, q_idx, w_idx, k_idx, kv_a, rel_bias, n_keep):
    B, S = kv_a.shape[:2]
    b_ix = jnp.arange(B)[:, None, None]

    def gather_rows(sel):
        return kv_a[b_ix, sel]

    def block(i):
        start = i * Q_BLOCK
        pos_q = start + jnp.arange(Q_BLOCK, dtype=jnp.int32)
        sl = lambda t: lax.dynamic_slice_in_dim(t, start, Q_BLOCK, axis=1)
        return dsa_core(sl(q_a), sl(q_idx), sl(w_idx), pos_q, k_idx, gather_rows, n_keep, rel_bias)

    out = lax.map(block, jnp.arange(S // Q_BLOCK, dtype=jnp.int32))
    return jnp.moveaxis(out, 0, 1).reshape(B, S, H_A, DH_A)


def dsa_sample(q_a, q_idx, w_idx, pos_q, k_idx_new, kv_new, cache_a_kv, cache_a_kidx, page_table, l, rel_bias, n_keep):
    DB, n_pages = page_table.shape
    past_len = n_pages * PAGE_SIZE
    T = kv_new.shape[1]
    past_kidx = cache_a_kidx[l, page_table].reshape(DB, past_len, D_IDX)
    k_idx_keys = jnp.concatenate([past_kidx, k_idx_new], axis=1)
    b_ix = jnp.arange(DB)[:, None, None]

    def gather_rows(sel):
        past_i = jnp.minimum(sel, past_len - 1)
        phys = page_table[b_ix, past_i // PAGE_SIZE]
        rows_past = cache_a_kv[l, phys, past_i % PAGE_SIZE]
        rows_new = kv_new[b_ix, jnp.clip(sel - past_len, 0, T - 1)]
        return jnp.where((sel >= past_len)[..., None], rows_new, rows_past)

    return dsa_core(q_a, q_idx, w_idx, pos_q, k_idx_keys, gather_rows, n_keep, rel_bias)


def mla_core(q_cat, pos_q, segments):
    logits = []
    for rows, pos_k in segments:
        lg = jnp.einsum('bqhc,blc->bhql', q_cat, rows).astype(jnp.float32)
        logits.append(jnp.where((pos_k[None, :] <= pos_q[:, None])[None, None], lg, -jnp.inf))
    p = jax.nn.softmax(jnp.concatenate(logits, axis=-1), axis=-1)
    out, start = 0.0, 0
    for rows, pos_k in segments:
        n = rows.shape[1]
        out = out + jnp.einsum('bhql,blc->bqhc', p[..., start:start + n].astype(rows.dtype), rows)
        start += n
    return out[..., :KV_LORA]


def mla_prompt(q_cat, kv_b):
    B, S = kv_b.shape[:2]
    pos_k = jnp.arange(S, dtype=jnp.int32)

    def block(i):
        start = i * Q_BLOCK
        q = lax.dynamic_slice_in_dim(q_cat, start, Q_BLOCK, axis=1)
        pos_q = start + jnp.arange(Q_BLOCK, dtype=jnp.int32)
        return mla_core(q, pos_q, ((kv_b, pos_k),))

    out = lax.map(block, jnp.arange(S // Q_BLOCK, dtype=jnp.int32))
    return jnp.moveaxis(out, 0, 1).reshape(B, S, H_B, KV_LORA)


def mla_sample(q_cat, pos_q, kv_new, cache_b_latent, page_table, l):
    DB, n_pages = page_table.shape
    past_len = n_pages * PAGE_SIZE
    past = cache_b_latent[l, page_table].reshape(DB, past_len, C_B)
    pos_past = jnp.arange(past_len, dtype=jnp.int32)
    return mla_core(q_cat, pos_q, ((past, pos_past), (kv_new, pos_q)))


def merge_heads(a, o_lat, w_uv_l, w_out_l):
    B, L = a.shape[:2]
    b = jnp.einsum('blhr,rhv->blhv', o_lat, w_uv_l)
    mix = jnp.concatenate([a.reshape(B, L, H_A * DH_A), b.reshape(B, L, H_B * D_V)], axis=-1)
    return mix @ w_out_l


def swiglu(x, w_gate, w_up, w_down):
    return (jax.nn.silu(x @ w_gate) * (x @ w_up)) @ w_down


def moe_swiglu(x, w_router_i, b_router_i, w_gate_i, w_up_i, w_down_i):
    logits = (x @ w_router_i).astype(jnp.float32) + b_router_i.astype(jnp.float32)
    top_val, top_idx = lax.top_k(logits, TOP_K)
    gate_top = jax.nn.softmax(top_val, axis=-1)
    gates = jnp.sum(jax.nn.one_hot(top_idx, N_EXPERTS, dtype=jnp.float32) * gate_top[..., None], axis=-2)
    out = jnp.zeros_like(x)
    for e in range(N_EXPERTS):
        out = out + gates[..., e:e + 1].astype(x.dtype) * swiglu(x, w_gate_i[e], w_up_i[e], w_down_i[e])
    return out


def channel_mixer(x, l, w_gate_dense, w_up_dense, w_down_dense, w_router, b_router, w_gate_exp, w_up_exp, w_down_exp):
    i = l // 2
    if l % 2 == 0:
        return swiglu(x, w_gate_dense[i], w_up_dense[i], w_down_dense[i])
    return moe_swiglu(x, w_router[i], b_router[i], w_gate_exp[i], w_up_exp[i], w_down_exp[i])


def setup_inputs(seed: int = 0) -> dict:
    keys = list(jax.random.split(jax.random.key(seed), 32))

    def nrm(shape, scale):
        return jax.random.normal(keys.pop(), shape, jnp.float32) * scale

    def gain(shape):
        return 1.0 + nrm(shape, 0.02)

    n_pages = PAST_LEN // PAGE_SIZE
    n_used = DEC_BATCH * n_pages
    n_pool = n_used + max(1, n_used // 4)
    x_prompt = nrm((BATCH, SEQ, D_MODEL), 1.0)
    x_sample = nrm((DEC_BATCH, DEC_SEQ, D_MODEL), 1.0)
    cache_a_kv = jax.random.normal(keys.pop(), (DEPTH, n_pool, PAGE_SIZE, C_A), jnp.float32)
    cache_a_kidx = jax.random.normal(keys.pop(), (DEPTH, n_pool, PAGE_SIZE, D_IDX), jnp.float32)
    cache_b_latent = jax.random.normal(keys.pop(), (DEPTH, n_pool, PAGE_SIZE, C_B), jnp.float32)
    perm = jax.random.permutation(keys.pop(), n_pool)
    page_table = perm[:n_used].reshape(DEC_BATCH, n_pages).astype(jnp.int32)
    return {
        'x_prompt': x_prompt,
        'x_sample': x_sample,
        'cache_a_kv': cache_a_kv,
        'cache_a_kidx': cache_a_kidx,
        'cache_b_latent': cache_b_latent,
        'page_table': page_table,
        'rel_bias': nrm((N_BUCKETS, H_A), 0.5),
        'w_in': nrm((DEPTH, D_MODEL, D_IN), D_MODEL ** -0.5),
        'idx_ln_g': gain((DEPTH, D_IDX)),
        'idx_ln_b': nrm((DEPTH, D_IDX), 0.02),
        'q_norm_g': gain((DEPTH, Q_LORA)),
        'w_uq': nrm((DEPTH, Q_LORA, H_B * (D_NOPE + D_ROPE)), Q_LORA ** -0.5),
        'kv_norm_g': gain((DEPTH, KV_LORA)),
        'w_uk': nrm((DEPTH, KV_LORA, H_B, D_NOPE), KV_LORA ** -0.5),
        'w_uv': nrm((DEPTH, KV_LORA, H_B, D_V), KV_LORA ** -0.5),
        'w_out': nrm((DEPTH, MIX_WIDTH, D_MODEL), BETA * MIX_WIDTH ** -0.5),
        'ln1_g': gain((DEPTH, D_MODEL)),
        'ln1_b': nrm((DEPTH, D_MODEL), 0.02),
        'ln2_g': gain((DEPTH, D_MODEL)),
        'ln2_b': nrm((DEPTH, D_MODEL), 0.02),
        'w_gate_dense': nrm((N_DENSE, D_MODEL, D_FF), D_MODEL ** -0.5),
        'w_up_dense': nrm((N_DENSE, D_MODEL, D_FF), D_MODEL ** -0.5),
        'w_down_dense': nrm((N_DENSE, D_FF, D_MODEL), BETA * D_FF ** -0.5),
        'w_router': nrm((N_MOE, D_MODEL, N_EXPERTS), D_MODEL ** -0.5),
        'b_router': nrm((N_MOE, N_EXPERTS), 0.01),
        'w_gate_exp': nrm((N_MOE, N_EXPERTS, D_MODEL, D_FF_EXP), D_MODEL ** -0.5),
        'w_up_exp': nrm((N_MOE, N_EXPERTS, D_MODEL, D_FF_EXP), D_MODEL ** -0.5),
        'w_down_exp': nrm((N_MOE, N_EXPERTS, D_FF_EXP, D_MODEL), BETA * D_FF_EXP ** -0.5),
    }


def reference(x_prompt, x_sample, cache_a_kv, cache_a_kidx, cache_b_latent, page_table, rel_bias, w_in, idx_ln_g, idx_ln_b, q_norm_g, w_uq, kv_norm_g, w_uk, w_uv, w_out, ln1_g, ln1_b, ln2_g, ln2_b, w_gate_dense, w_up_dense, w_down_dense, w_router, b_router, w_gate_exp, w_up_exp, w_down_exp):
    S = x_prompt.shape[1]
    T = x_sample.shape[1]
    past_len = page_table.shape[1] * PAGE_SIZE
    pos_p = jnp.arange(S, dtype=jnp.int32)
    pos_s = past_len + jnp.arange(T, dtype=jnp.int32)
    keep_p = min(TOPK_MAX, S // 4)
    keep_s = min(TOPK_MAX, (past_len + T) // 4)
    xp, xs = x_prompt, x_sample
    kva_p_all, kid_p_all, lat_p_all = [], [], []
    kva_s_all, kid_s_all, lat_s_all = [], [], []
    for l in range(DEPTH):
        mix_w = (w_in[l], idx_ln_g[l], idx_ln_b[l], q_norm_g[l], w_uq[l], kv_norm_g[l], w_uk[l])
        qa_p, kva_p, qi_p, ki_p, wi_p, qc_p, kvb_p = mixer_projections(xp, pos_p, *mix_w)
        qa_s, kva_s, qi_s, ki_s, wi_s, qc_s, kvb_s = mixer_projections(xs, pos_s, *mix_w)
        a_p = dsa_prompt(qa_p, qi_p, wi_p, ki_p, kva_p, rel_bias, keep_p)
        o_p = mla_prompt(qc_p, kvb_p)
        a_s = dsa_sample(qa_s, qi_s, wi_s, pos_s, ki_s, kva_s, cache_a_kv, cache_a_kidx, page_table, l, rel_bias, keep_s)
        o_s = mla_sample(qc_s, pos_s, kvb_s, cache_b_latent, page_table, l)
        xp = layer_norm(ALPHA * xp + merge_heads(a_p, o_p, w_uv[l], w_out[l]), ln1_g[l], ln1_b[l])
        xs = layer_norm(ALPHA * xs + merge_heads(a_s, o_s, w_uv[l], w_out[l]), ln1_g[l], ln1_b[l])
        ffn_w = (w_gate_dense, w_up_dense, w_down_dense, w_router, b_router, w_gate_exp, w_up_exp, w_down_exp)
        xp = layer_norm(ALPHA * xp + channel_mixer(xp, l, *ffn_w), ln2_g[l], ln2_b[l])
        xs = layer_norm(ALPHA * xs + channel_mixer(xs, l, *ffn_w), ln2_g[l], ln2_b[l])
        kva_p_all.append(kva_p)
        kid_p_all.append(ki_p)
        lat_p_all.append(kvb_p)
        kva_s_all.append(kva_s)
        kid_s_all.append(ki_s)
        lat_s_all.append(kvb_s)
    return (xp, xs, jnp.stack(kva_p_all), jnp.stack(kid_p_all), jnp.stack(lat_p_all), jnp.stack(kva_s_all), jnp.stack(kid_s_all), jnp.stack(lat_s_all))
```

```python
import functools
import math

import jax
import jax.numpy as jnp
from jax import lax
from jax.experimental import pallas as pl
from jax.experimental.pallas import tpu as pltpu

F32 = jnp.float32
BF16 = jnp.bfloat16
I32 = jnp.int32

D_MODEL = 1024
DEPTH = 4
PAGE_SIZE = 128
H_A = 4
DH_A = 128
H_IDX = 8
D_IDX = 64
TOPK_MAX = 256
N_BUCKETS = 32
T5_MAX_DIST = 128
H_B = 4
D_NOPE = 128
D_ROPE = 64
D_V = 128
Q_LORA = 384
KV_LORA = 256
ROPE_THETA = 10000.0
C_A = 2 * DH_A
C_B = KV_LORA + D_ROPE
N_EXPERTS = 8
ALPHA = (2 * DEPTH) ** 0.25
EPS = 1e-5
Q_BLOCK = 128
MLA_SCALE = (D_NOPE + D_ROPE) ** -0.5

LANES = 128
TOKEN_BLOCK = 512
KEY_CHUNK = 256
PAGES_PER_STEP = 16
VMEM_LIMIT = 56 * 1024 * 1024

INT_MIN = -(2 ** 31)
NEG_BIG = -1e30
NT_DIMS = (((1,), (1,)), ((), ()))


def _cparams(n_axes):
    return pltpu.CompilerParams(dimension_semantics=("arbitrary",) * n_axes,
                                vmem_limit_bytes=VMEM_LIMIT)


def _dot(a, b):
    return jnp.dot(a, b, preferred_element_type=F32)


def _dot_nt(a, b):
    return lax.dot_general(a, b, NT_DIMS, preferred_element_type=F32)


def _layer_norm(x, g, b):
    xc = x - jnp.mean(x, axis=-1, keepdims=True)
    y = xc * lax.rsqrt(jnp.mean(xc * xc, axis=-1, keepdims=True) + EPS)
    return y * g + b


def _rms_norm(x, g):
    return x * lax.rsqrt(jnp.mean(x * x, axis=-1, keepdims=True) + EPS) * g


def _order_key(score):
    bits = lax.bitcast_convert_type(score, I32)
    return jnp.where(bits < 0, bits ^ jnp.int32(0x7FFFFFFF), bits)


def _bias_of_dist(dist, relb_ref, h):
    n = jnp.maximum(dist, 0)
    max_exact = N_BUCKETS // 2
    nf = jnp.maximum(n, max_exact).astype(F32)
    large = max_exact + (jnp.log(nf / max_exact) / math.log(T5_MAX_DIST / max_exact)
                         * (N_BUCKETS - max_exact)).astype(I32)
    bucket = jnp.where(n < max_exact, n, jnp.minimum(large, N_BUCKETS - 1))
    out = jnp.zeros(dist.shape, F32)
    for b in range(N_BUCKETS):
        out = jnp.where(bucket == b, relb_ref[b, h], out)
    return out


def _bias_tables_kernel(relb_ref, tile_ref, past_ref, new_ref, *, past_len, t_new):
    r = lax.broadcasted_iota(I32, (Q_BLOCK, KEY_CHUNK), 0)
    c = lax.broadcasted_iota(I32, (Q_BLOCK, KEY_CHUNK), 1)
    for m in range(4):
        for h in range(H_A):
            tile_ref[m, h] = _bias_of_dist(Q_BLOCK * m + r - c, relb_ref, h)
    q = lax.broadcasted_iota(I32, (t_new, past_len), 0)
    s = lax.broadcasted_iota(I32, (t_new, past_len), 1)
    qn = lax.broadcasted_iota(I32, (t_new, LANES), 0)
    sn = lax.broadcasted_iota(I32, (t_new, LANES), 1)
    for h in range(H_A):
        past_ref[h * t_new:(h + 1) * t_new, :] = _bias_of_dist(past_len + q - s, relb_ref, h)
        new_ref[h * t_new:(h + 1) * t_new, :] = _bias_of_dist(qn - sn, relb_ref, h)


def _bias_tables(rel_bias, past_len, t_new):
    return pl.pallas_call(
        functools.partial(_bias_tables_kernel, past_len=past_len, t_new=t_new),
        out_shape=(jax.ShapeDtypeStruct((4, H_A, Q_BLOCK, KEY_CHUNK), F32),
                   jax.ShapeDtypeStruct((H_A * t_new, past_len), F32),
                   jax.ShapeDtypeStruct((H_A * t_new, LANES), F32)),
        in_specs=[pl.BlockSpec(memory_space=pltpu.SMEM)],
        name="bias_tables",
    )(rel_bias)


def _proj_kernel(x_ref, cos_ref, sin_ref, wqa, wkva, wqidx, wkidx, wwidx, wcq, wckv, wkr,
                 idxg, idxb, qng, kvng, wuqn, wuqr, wuqrs, wukt,
                 qa_o, kva_o, kvabf_o, qidx_o, kidx_o, kidxbf_o, widx_o, qcat_o, kvb_o, kvbbf_o):
    x = x_ref[...].astype(BF16)
    cos = cos_ref[...]
    sin = sin_ref[...]
    qa_o[...] = (_dot(x, wqa[...]) * (DH_A ** -0.5)).astype(BF16)
    kva = _dot(x, wkva[...])
    kva_o[...] = kva
    kvabf_o[...] = kva.astype(BF16)
    for h in range(H_IDX):
        qidx_o[h] = (_dot(x, wqidx[h]) * (D_IDX ** -0.5)).astype(BF16)
    kidx = _layer_norm(_dot(x, wkidx[...]), idxg[...], idxb[...])
    kidx_o[...] = kidx
    kidxbf_o[...] = kidx.astype(BF16)
    widx_o[...] = _dot(x, wwidx[...])[:, :H_IDX] * (H_IDX ** -0.5)
    cq = _rms_norm(_dot(x, wcq[...]), qng[...]).astype(BF16)
    for h in range(H_B):
        q_nope = _dot(cq, wuqn[h]).astype(BF16)
        q_lat = _dot(q_nope, wukt[h])
        q_rope = _dot(cq, wuqr[h]) * cos + _dot(cq, wuqrs[h]) * sin
        qcat_o[h, :, 0:KV_LORA] = (q_lat * MLA_SCALE).astype(BF16)
        qcat_o[h, :, KV_LORA:C_B] = (q_rope * MLA_SCALE).astype(BF16)
    ckv = _rms_norm(_dot(x, wckv[...]), kvng[...])
    k_rope = _dot(x, wkr[0]) * cos + _dot(x, wkr[1]) * sin
    kvb_o[:, 0:KV_LORA] = ckv
    kvb_o[:, KV_LORA:C_B] = k_rope
    kvbbf_o[:, 0:KV_LORA] = ckv.astype(BF16)
    kvbbf_o[:, KV_LORA:C_B] = k_rope.astype(BF16)


def _projections(x, cos_t, sin_t, w, l):
    n = x.shape[0]
    tm = TOKEN_BLOCK
    row = lambda width: pl.BlockSpec((tm, width), lambda i: (i, 0))
    lay2 = lambda a: pl.BlockSpec((None,) + a.shape[1:], lambda i: (l,) + (0,) * (a.ndim - 1))
    weights = [w["wqa"], w["wkva"], w["wqidx"], w["wkidx"], w["wwidx"], w["wcq"], w["wckv"], w["wkr"],
               w["idxg"], w["idxb"], w["qng"], w["kvng"], w["wuqn"], w["wuqr"], w["wuqrs"], w["wukt"]]
    out_shape = (
        jax.ShapeDtypeStruct((n, H_A * DH_A), BF16),
        jax.ShapeDtypeStruct((n, C_A), F32),
        jax.ShapeDtypeStruct((n, C_A), BF16),
        jax.ShapeDtypeStruct((H_IDX, n, D_IDX), BF16),
        jax.ShapeDtypeStruct((n, D_IDX), F32),
        jax.ShapeDtypeStruct((n, D_IDX), BF16),
        jax.ShapeDtypeStruct((n, H_IDX), F32),
        jax.ShapeDtypeStruct((H_B, n, C_B), BF16),
        jax.ShapeDtypeStruct((n, C_B), F32),
        jax.ShapeDtypeStruct((n, C_B), BF16),
    )
    out_specs = (row(H_A * DH_A), row(C_A), row(C_A),
                 pl.BlockSpec((H_IDX, tm, D_IDX), lambda i: (0, i, 0)),
                 row(D_IDX), row(D_IDX), row(H_IDX),
                 pl.BlockSpec((H_B, tm, C_B), lambda i: (0, i, 0)),
                 row(C_B), row(C_B))
    return pl.pallas_call(
        _proj_kernel,
        grid=(n // tm,),
        in_specs=[row(D_MODEL), row(D_ROPE), row(D_ROPE)] + [lay2(a) for a in weights],
        out_specs=out_specs,
        out_shape=out_shape,
        compiler_params=_cparams(1),
        name="projections",
    )(x, cos_t, sin_t, *weights)


def _select_threshold(keys_ref, n_rows, n_chunks, chunk, n_keep, col_bits):
    def count(pred):
        def body(j, c):
            off = pl.multiple_of(j * chunk, chunk)
            return c + jnp.where(pred(keys_ref[:, pl.ds(off, chunk)], off), 1.0, 0.0)
        c = lax.fori_loop(0, n_chunks, body, jnp.zeros((n_rows, chunk), F32))
        return jnp.sum(c, axis=1, keepdims=True)

    def bit_body(p, prefix):
        cand = prefix + lax.shift_left(jnp.int32(1), 31 - p)
        cnt = count(lambda t, off: t >= cand)
        return jnp.where(cnt >= n_keep, cand, prefix)

    thr = lax.fori_loop(0, 32, bit_body, jnp.full((n_rows, 1), INT_MIN, I32))
    thr = jnp.maximum(thr, INT_MIN + 1)
    n_gt = count(lambda t, off: t > thr)
    n_eq = count(lambda t, off: t == thr)
    need = n_keep - n_gt
    tie = jnp.max(jnp.where(n_eq > need, 1.0, 0.0)) > 0.0

    @pl.when(tie)
    def _():
        def col_of(off):
            return off + lax.broadcasted_iota(I32, (n_rows, chunk), 1)

        def cut_body(p, cut):
            cand = cut + lax.shift_left(jnp.int32(1), col_bits - 1 - p)
            cnt = count(lambda t, off: (t == thr) & (col_of(off) < cand))
            return jnp.where(cnt < need, cand, cut)

        cut = lax.fori_loop(0, col_bits, cut_body, jnp.zeros((n_rows, 1), I32))

        def demote(j, carry):
            off = pl.multiple_of(j * chunk, chunk)
            t = keys_ref[:, pl.ds(off, chunk)]
            lose = (t == thr) & (col_of(off) > cut)
            keys_ref[:, pl.ds(off, chunk)] = jnp.where(lose, thr - 1, t)
            return carry

        lax.fori_loop(0, n_chunks, demote, 0)

    return thr


def _dsa_prompt_kernel(qidx_ref, widx_ref, qa_ref, kidx_ref, kva_ref, bias_ref, out_ref,
                       keys_ref, wb_ref, m_ref, l_ref, acc_ref, *, n_keep, col_bits):
    i = pl.program_id(1)
    kc = KEY_CHUNK
    blocks_per_chunk = kc // Q_BLOCK
    n_chunks = i // blocks_per_chunk + 1
    row_pos = i * Q_BLOCK + lax.broadcasted_iota(I32, (Q_BLOCK, kc), 0)
    lane = lax.broadcasted_iota(I32, (Q_BLOCK, kc), 1)

    for h in range(H_IDX):
        wb_ref[h] = jnp.broadcast_to(widx_ref[:, h:h + 1], (Q_BLOCK, kc))

    def score_body(j, carry):
        off = pl.multiple_of(j * kc, kc)
        keys_c = kidx_ref[pl.ds(off, kc), :]
        score = jnp.zeros((Q_BLOCK, kc), F32)
        for h in range(H_IDX):
            score = score + wb_ref[h] * jnp.maximum(_dot_nt(qidx_ref[h], keys_c), 0.0)
        keys_ref[:, pl.ds(off, kc)] = jnp.where(off + lane <= row_pos, _order_key(score), INT_MIN)
        return carry

    lax.fori_loop(0, n_chunks, score_body, 0)
    thr = _select_threshold(keys_ref, Q_BLOCK, n_chunks, kc, n_keep, col_bits)

    m_ref[...] = jnp.full(m_ref.shape, NEG_BIG, F32)
    l_ref[...] = jnp.zeros(l_ref.shape, F32)
    acc_ref[...] = jnp.zeros(acc_ref.shape, F32)

    def att_body(j, carry):
        off = pl.multiple_of(j * kc, kc)
        kv = kva_ref[pl.ds(off, kc), :]
        k = kv[:, :DH_A]
        v = kv[:, DH_A:]
        sel = keys_ref[:, pl.ds(off, kc)] >= thr
        m_idx = jnp.minimum(i - blocks_per_chunk * j, 3)
        for h in range(H_A):
            s = _dot_nt(qa_ref[:, h * DH_A:(h + 1) * DH_A], k) + bias_ref[m_idx, h]
            s = jnp.where(sel, s, NEG_BIG)
            m_old = m_ref[h]
            m_new = jnp.maximum(m_old, jnp.max(s, axis=1, keepdims=True))
            p = jnp.where(sel, jnp.exp(s - m_new), 0.0)
            alpha = jnp.exp(m_old - m_new)
            l_ref[h] = alpha * l_ref[h] + jnp.sum(p, axis=1, keepdims=True)
            acc_ref[h] = alpha * acc_ref[h] + _dot(p.astype(BF16), v)
            m_ref[h] = m_new
        return carry

    lax.fori_loop(0, n_chunks, att_body, 0)
    for h in range(H_A):
        out_ref[:, h * DH_A:(h + 1) * DH_A] = acc_ref[h] / l_ref[h]


def _dsa_prompt(qa, qidx, widx, kidx_bf, kva_bf, bias_tile, batch, seq):
    nq = seq // Q_BLOCK
    n_keep = min(TOPK_MAX, seq // 4)
    kern = functools.partial(_dsa_prompt_kernel, n_keep=n_keep, col_bits=max(1, (seq - 1).bit_length()))
    return pl.pallas_call(
        kern,
        grid=(batch, nq),
        in_specs=[
            pl.BlockSpec((H_IDX, Q_BLOCK, D_IDX), lambda b, i: (0, b * nq + i, 0)),
            pl.BlockSpec((Q_BLOCK, H_IDX), lambda b, i: (b * nq + i, 0)),
            pl.BlockSpec((Q_BLOCK, H_A * DH_A), lambda b, i: (b * nq + i, 0)),
            pl.BlockSpec((seq, D_IDX), lambda b, i: (b, 0)),
            pl.BlockSpec((seq, C_A), lambda b, i: (b, 0)),
            pl.BlockSpec(bias_tile.shape, lambda b, i: (0, 0, 0, 0)),
        ],
        out_specs=pl.BlockSpec((Q_BLOCK, H_A * DH_A), lambda b, i: (b * nq + i, 0)),
        out_shape=jax.ShapeDtypeStruct((batch * seq, H_A * DH_A), F32),
        scratch_shapes=[
            pltpu.VMEM((Q_BLOCK, seq), I32),
            pltpu.VMEM((H_IDX, Q_BLOCK, KEY_CHUNK), F32),
            pltpu.VMEM((H_A, Q_BLOCK, 1), F32),
            pltpu.VMEM((H_A, Q_BLOCK, 1), F32),
            pltpu.VMEM((H_A, Q_BLOCK, DH_A), F32),
        ],
        compiler_params=_cparams(2),
        name="dsa_prompt",
    )(qidx, widx, qa, kidx_bf, kva_bf, bias_tile)


def _mla_prompt_kernel(q_ref, kvb_ref, out_ref, m_ref, l_ref, acc_ref):
    i = pl.program_id(1)
    kc = KEY_CHUNK
    rows = H_B * Q_BLOCK
    n_chunks = i // (kc // Q_BLOCK) + 1
    q = q_ref[...].reshape(rows, C_B)
    row_pos = i * Q_BLOCK + lax.broadcasted_iota(I32, (H_B, Q_BLOCK, kc), 1).reshape(rows, kc)
    lane = lax.broadcasted_iota(I32, (rows, kc), 1)
    m_ref[...] = jnp.full(m_ref.shape, NEG_BIG, F32)
    l_ref[...] = jnp.zeros(l_ref.shape, F32)
    acc_ref[...] = jnp.zeros(acc_ref.shape, F32)

    def body(j, carry):
        off = pl.multiple_of(j * kc, kc)
        kv = kvb_ref[pl.ds(off, kc), :]
        vis = off + lane <= row_pos
        s = jnp.where(vis, _dot_nt(q, kv), NEG_BIG)
        m_old = m_ref[...]
        m_new = jnp.maximum(m_old, jnp.max(s, axis=1, keepdims=True))
        p = jnp.where(vis, jnp.exp(s - m_new), 0.0)
        alpha = jnp.exp(m_old - m_new)
        l_ref[...] = alpha * l_ref[...] + jnp.sum(p, axis=1, keepdims=True)
        acc_ref[...] = alpha * acc_ref[...] + _dot(p.astype(BF16), kv[:, :KV_LORA])
        m_ref[...] = m_new
        return carry

    lax.fori_loop(0, n_chunks, body, 0)
    out_ref[...] = (acc_ref[...] / l_ref[...]).reshape(H_B, Q_BLOCK, KV_LORA)


def _mla_prompt(qcat, kvb_bf, batch, seq):
    nq = seq // Q_BLOCK
    rows = H_B * Q_BLOCK
    return pl.pallas_call(
        _mla_prompt_kernel,
        grid=(batch, nq),
        in_specs=[
            pl.BlockSpec((H_B, Q_BLOCK, C_B), lambda b, i: (0, b * nq + i, 0)),
            pl.BlockSpec((seq, C_B), lambda b, i: (b, 0)),
        ],
        out_specs=pl.BlockSpec((H_B, Q_BLOCK, KV_LORA), lambda b, i: (0, b * nq + i, 0)),
        out_shape=jax.ShapeDtypeStruct((H_B, batch * seq, KV_LORA), F32),
        scratch_shapes=[
            pltpu.VMEM((rows, 1), F32),
            pltpu.VMEM((rows, 1), F32),
            pltpu.VMEM((rows, KV_LORA), F32),
        ],
        compiler_params=_cparams(2),
        name="mla_prompt",
    )(qcat, kvb_bf)


def _page_specs(l, width, pps):
    def spec(p):
        return pl.BlockSpec((None, None, PAGE_SIZE, width),
                            lambda b, c, pt: (l, pt[b, c * pps + p], 0, 0))
    return [spec(p) for p in range(pps)]


def _samp_idx_kernel(pt_ref, qidx_ref, widx_ref, knew_ref, *rest, pps, past_len, t_new, n_keep, col_bits):
    pages = rest[:pps]
    sel_ref = rest[pps]
    keys_ref = rest[pps + 1]
    c = pl.program_id(1)
    n_steps = pl.num_programs(1)
    total = past_len + LANES
    q = qidx_ref[...].reshape(H_IDX * t_new, D_IDX).astype(BF16)
    wb = [jnp.broadcast_to(widx_ref[:, h:h + 1], (t_new, LANES)) for h in range(H_IDX)]

    def scores(keys_c):
        r = jnp.maximum(_dot_nt(q, keys_c), 0.0)
        score = jnp.zeros((t_new, LANES), F32)
        for h in range(H_IDX):
            score = score + wb[h] * r[h * t_new:(h + 1) * t_new, :]
        return score

    for p in range(pps):
        off = pl.multiple_of((c * pps + p) * PAGE_SIZE, PAGE_SIZE)
        keys_ref[:, pl.ds(off, PAGE_SIZE)] = _order_key(scores(pages[p][...].astype(BF16)))

    @pl.when(c == n_steps - 1)
    def _():
        qi = lax.broadcasted_iota(I32, (t_new, LANES), 0)
        ki = lax.broadcasted_iota(I32, (t_new, LANES), 1)
        key_new = jnp.where(ki <= qi, _order_key(scores(knew_ref[...])), INT_MIN)
        keys_ref[:, past_len:total] = key_new
        thr = _select_threshold(keys_ref, t_new, total // LANES, LANES, n_keep, col_bits)
        sel_ref[...] = jnp.where(keys_ref[...] >= thr, 1.0, 0.0)


def _samp_idx(page_table, qidx, widx, kidx_new_pad, cache, l, n_prompt, t_new):
    db, n_pages = page_table.shape
    pps = min(PAGES_PER_STEP, n_pages)
    past_len = n_pages * PAGE_SIZE
    total = past_len + LANES
    n_keep = min(TOPK_MAX, (past_len + t_new) // 4)
    base = n_prompt // t_new
    kern = functools.partial(_samp_idx_kernel, pps=pps, past_len=past_len, t_new=t_new, n_keep=n_keep,
                             col_bits=max(1, (total - 1).bit_length()))
    grid_spec = pltpu.PrefetchScalarGridSpec(
        num_scalar_prefetch=1,
        grid=(db, n_pages // pps),
        in_specs=[
            pl.BlockSpec((H_IDX, t_new, D_IDX), lambda b, c, pt: (0, b, 0)),
            pl.BlockSpec((t_new, H_IDX), lambda b, c, pt: (base + b, 0)),
            pl.BlockSpec((None, LANES, D_IDX), lambda b, c, pt: (b, 0, 0)),
        ] + _page_specs(l, D_IDX, pps),
        out_specs=pl.BlockSpec((None, t_new, total), lambda b, c, pt: (b, 0, 0)),
        scratch_shapes=[pltpu.VMEM((t_new, total), I32)],
    )
    return pl.pallas_call(
        kern, grid_spec=grid_spec,
        out_shape=jax.ShapeDtypeStruct((db, t_new, total), F32),
        compiler_params=_cparams(2), name="sample_indexer",
    )(page_table, qidx, widx, kidx_new_pad, *([cache] * pps))


def _flash_update(s, valid, v, m_ref, l_ref, acc_ref):
    if valid is not None:
        s = jnp.where(valid, s, NEG_BIG)
    m_old = m_ref[...]
    m_new = jnp.maximum(m_old, jnp.max(s, axis=1, keepdims=True))
    p = jnp.exp(s - m_new)
    if valid is not None:
        p = jnp.where(valid, p, 0.0)
    alpha = jnp.exp(m_old - m_new)
    l_ref[...] = alpha * l_ref[...] + jnp.sum(p, axis=1, keepdims=True)
    acc_ref[...] = alpha * acc_ref[...] + _dot(p.astype(BF16), v)
    m_ref[...] = m_new


def _samp_dsa_kernel(pt_ref, qa_ref, sel_ref, kvnew_ref, biasp_ref, biasn_ref, *rest, pps, past_len, t_new):
    pages = rest[:pps]
    out_ref, kbuf, m_ref, l_ref, acc_ref = rest[pps:]
    c = pl.program_id(1)
    n_steps = pl.num_programs(1)
    span = pps * PAGE_SIZE

    @pl.when(c == 0)
    def _():
        m_ref[...] = jnp.full(m_ref.shape, NEG_BIG, F32)
        l_ref[...] = jnp.zeros(l_ref.shape, F32)
        acc_ref[...] = jnp.zeros(acc_ref.shape, F32)

    qa = qa_ref[...]
    q = jnp.concatenate([qa[:, h * DH_A:(h + 1) * DH_A] for h in range(H_A)], axis=0).astype(BF16)
    for p in range(pps):
        kbuf[p * PAGE_SIZE:(p + 1) * PAGE_SIZE, :] = pages[p][...].astype(BF16)
    off = pl.multiple_of(c * span, span)
    sel = sel_ref[:, pl.ds(off, span)]
    valid = jnp.concatenate([sel] * H_A, axis=0) > 0.5
    s = _dot_nt(q, kbuf[:, :DH_A]) + biasp_ref[:, pl.ds(off, span)]
    _flash_update(s, valid, kbuf[:, DH_A:], m_ref, l_ref, acc_ref)

    @pl.when(c == n_steps - 1)
    def _():
        kvn = kvnew_ref[...]
        seln = sel_ref[:, past_len:past_len + LANES]
        validn = jnp.concatenate([seln] * H_A, axis=0) > 0.5
        sn = _dot_nt(q, kvn[:, :DH_A]) + biasn_ref[...]
        _flash_update(sn, validn, kvn[:, DH_A:], m_ref, l_ref, acc_ref)
        o = acc_ref[...] / l_ref[...]
        for h in range(H_A):
            out_ref[:, h * DH_A:(h + 1) * DH_A] = o[h * t_new:(h + 1) * t_new, :]


def _samp_dsa(page_table, qa, sel, kva_new_pad, bias_past, bias_new, cache, l, t_new):
    db, n_pages = page_table.shape
    pps = min(PAGES_PER_STEP, n_pages)
    past_len = n_pages * PAGE_SIZE
    total = past_len + LANES
    rows = H_A * t_new
    kern = functools.partial(_samp_dsa_kernel, pps=pps, past_len=past_len, t_new=t_new)
    grid_spec = pltpu.PrefetchScalarGridSpec(
        num_scalar_prefetch=1,
        grid=(db, n_pages // pps),
        in_specs=[
            pl.BlockSpec((t_new, H_A * DH_A), lambda b, c, pt: (b, 0)),
            pl.BlockSpec((None, t_new, total), lambda b, c, pt: (b, 0, 0)),
            pl.BlockSpec((None, LANES, C_A), lambda b, c, pt: (b, 0, 0)),
            pl.BlockSpec((rows, past_len), lambda b, c, pt: (0, 0)),
            pl.BlockSpec((rows, LANES), lambda b, c, pt: (0, 0)),
        ] + _page_specs(l, C_A, pps),
        out_specs=pl.BlockSpec((t_new, H_A * DH_A), lambda b, c, pt: (b, 0)),
        scratch_shapes=[
            pltpu.VMEM((pps * PAGE_SIZE, C_A), BF16),
            pltpu.VMEM((rows, 1), F32),
            pltpu.VMEM((rows, 1), F32),
            pltpu.VMEM((rows, DH_A), F32),
        ],
    )
    return pl.pallas_call(
        kern, grid_spec=grid_spec,
        out_shape=jax.ShapeDtypeStruct((db * t_new, H_A * DH_A), F32),
        compiler_params=_cparams(2), name="sample_dsa",
    )(page_table, qa, sel, kva_new_pad, bias_past, bias_new, *([cache] * pps))


def _samp_mla_kernel(pt_ref, q_ref, kvnew_ref, *rest, pps, t_new):
    pages = rest[:pps]
    out_ref, kbuf, m_ref, l_ref, acc_ref = rest[pps:]
    c = pl.program_id(1)
    n_steps = pl.num_programs(1)
    rows = H_B * t_new

    @pl.when(c == 0)
    def _():
        m_ref[...] = jnp.full(m_ref.shape, NEG_BIG, F32)
        l_ref[...] = jnp.zeros(l_ref.shape, F32)
        acc_ref[...] = jnp.zeros(acc_ref.shape, F32)

    q = q_ref[...].reshape(rows, C_B).astype(BF16)
    for p in range(pps):
        kbuf[p * PAGE_SIZE:(p + 1) * PAGE_SIZE, :] = pages[p][...].astype(BF16)
    _flash_update(_dot_nt(q, kbuf[...]), None, kbuf[:, :KV_LORA], m_ref, l_ref, acc_ref)

    @pl.when(c == n_steps - 1)
    def _():
        kvn = kvnew_ref[...]
        qi = lax.broadcasted_iota(I32, (H_B, t_new, LANES), 1).reshape(rows, LANES)
        ki = lax.broadcasted_iota(I32, (rows, LANES), 1)
        _flash_update(_dot_nt(q, kvn), ki <= qi, kvn[:, :KV_LORA], m_ref, l_ref, acc_ref)
        out_ref[...] = (acc_ref[...] / l_ref[...]).reshape(H_B, t_new, KV_LORA)


def _samp_mla(page_table, qcat, kvb_new_pad, cache, l, t_new):
    db, n_pages = page_table.shape
    pps = min(PAGES_PER_STEP, n_pages)
    rows = H_B * t_new
    kern = functools.partial(_samp_mla_kernel, pps=pps, t_new=t_new)
    grid_spec = pltpu.PrefetchScalarGridSpec(
        num_scalar_prefetch=1,
        grid=(db, n_pages // pps),
        in_specs=[
            pl.BlockSpec((H_B, t_new, C_B), lambda b, c, pt: (0, b, 0)),
            pl.BlockSpec((None, LANES, C_B), lambda b, c, pt: (b, 0, 0)),
        ] + _page_specs(l, C_B, pps),
        out_specs=pl.BlockSpec((H_B, t_new, KV_LORA), lambda b, c, pt: (0, b, 0)),
        scratch_shapes=[
            pltpu.VMEM((pps * PAGE_SIZE, C_B), BF16),
            pltpu.VMEM((rows, 1), F32),
            pltpu.VMEM((rows, 1), F32),
            pltpu.VMEM((rows, KV_LORA), F32),
        ],
    )
    return pl.pallas_call(
        kern, grid_spec=grid_spec,
        out_shape=jax.ShapeDtypeStruct((H_B, db * t_new, KV_LORA), F32),
        compiler_params=_cparams(2), name="sample_mla",
    )(page_table, qcat, kvb_new_pad, *([cache] * pps))


def _merge_kernel(x_ref, a_ref, o_ref, wuv_ref, wout_ref, g_ref, b_ref, out_ref):
    width_a = H_A * DH_A
    mix = _dot(a_ref[...].astype(BF16), wout_ref[0:width_a, :])
    for h in range(H_B):
        b_h = _dot(o_ref[h].astype(BF16), wuv_ref[h]).astype(BF16)
        mix = mix + _dot(b_h, wout_ref[width_a + h * D_V:width_a + (h + 1) * D_V, :])
    out_ref[...] = _layer_norm(ALPHA * x_ref[...] + mix, g_ref[...], b_ref[...])


def _merge(x, a, o_lat, w, l):
    n = x.shape[0]
    tm = TOKEN_BLOCK
    row = lambda width: pl.BlockSpec((tm, width), lambda i: (i, 0))
    lay = lambda arr: pl.BlockSpec((None,) + arr.shape[1:], lambda i: (l,) + (0,) * (arr.ndim - 1))
    return pl.pallas_call(
        _merge_kernel,
        grid=(n // tm,),
        in_specs=[row(D_MODEL), row(H_A * DH_A), pl.BlockSpec((H_B, tm, KV_LORA), lambda i: (0, i, 0)),
                  lay(w["wuv"]), lay(w["wout"]), lay(w["ln1g"]), lay(w["ln1b"])],
        out_specs=row(D_MODEL),
        out_shape=jax.ShapeDtypeStruct((n, D_MODEL), F32),
        compiler_params=_cparams(1), name="merge_heads",
    )(x, a, o_lat, w["wuv"], w["wout"], w["ln1g"], w["ln1b"])


def _ffn_kernel(x_ref, wg_ref, wu_ref, wd_ref, g_ref, b_ref, out_ref, acc_ref):
    f = pl.program_id(1)

    @pl.when(f == 0)
    def _():
        acc_ref[...] = jnp.zeros(acc_ref.shape, F32)

    x = x_ref[...].astype(BF16)
    hid = jax.nn.silu(_dot(x, wg_ref[...])) * _dot(x, wu_ref[...])
    acc_ref[...] += _dot(hid.astype(BF16), wd_ref[...])

    @pl.when(f == pl.num_programs(1) - 1)
    def _():
        out_ref[...] = _layer_norm(ALPHA * x_ref[...] + acc_ref[...], g_ref[...], b_ref[...])


def _ff_chunk(d_ff):
    for parts in (2, 1, 11, 22):
        if d_ff % parts == 0 and (d_ff // parts) % LANES == 0:
            return d_ff // parts
    return d_ff


def _ffn_dense(x, w, i_dense, l):
    n = x.shape[0]
    tm = TOKEN_BLOCK
    d_ff = w["wg_d"].shape[-1]
    tf = _ff_chunk(d_ff)
    return pl.pallas_call(
        _ffn_kernel,
        grid=(n // tm, d_ff // tf),
        in_specs=[pl.BlockSpec((tm, D_MODEL), lambda i, f: (i, 0)),
                  pl.BlockSpec((None, D_MODEL, tf), lambda i, f: (i_dense, 0, f)),
                  pl.BlockSpec((None, D_MODEL, tf), lambda i, f: (i_dense, 0, f)),
                  pl.BlockSpec((None, tf, D_MODEL), lambda i, f: (i_dense, f, 0)),
                  pl.BlockSpec((None, 1, D_MODEL), lambda i, f: (l, 0, 0)),
                  pl.BlockSpec((None, 1, D_MODEL), lambda i, f: (l, 0, 0))],
        out_specs=pl.BlockSpec((tm, D_MODEL), lambda i, f: (i, 0)),
        out_shape=jax.ShapeDtypeStruct((n, D_MODEL), F32),
        scratch_shapes=[pltpu.VMEM((tm, D_MODEL), F32)],
        compiler_params=_cparams(2), name="ffn_dense",
    )(x, w["wg_d"], w["wu_d"], w["wd_d"], w["ln2g"], w["ln2b"])


def _router_kernel(x_ref, wr_ref, br_ref, gates_ref):
    logits = jnp.dot(x_ref[...], wr_ref[...], preferred_element_type=F32,
                     precision=lax.Precision.HIGHEST) + br_ref[...]
    lane = lax.broadcasted_iota(I32, logits.shape, 1).astype(F32)
    logits = jnp.where(lane < N_EXPERTS, logits, -jnp.inf)
    m1 = jnp.max(logits, axis=1, keepdims=True)
    i1 = jnp.min(jnp.where(logits == m1, lane, float(LANES)), axis=1, keepdims=True)
    rest = jnp.where(lane == i1, -jnp.inf, logits)
    m2 = jnp.max(rest, axis=1, keepdims=True)
    i2 = jnp.min(jnp.where(rest == m2, lane, float(LANES)), axis=1, keepdims=True)
    e2 = jnp.exp(m2 - m1)
    g1 = 1.0 / (1.0 + e2)
    g2 = e2 / (1.0 + e2)
    for e in range(N_EXPERTS):
        gate = jnp.where(i1 == e, g1, 0.0) + jnp.where(i2 == e, g2, 0.0)
        gates_ref[e] = jnp.broadcast_to(gate, (gate.shape[0], LANES))


def _router(x, w, i_moe):
    n = x.shape[0]
    tm = TOKEN_BLOCK
    return pl.pallas_call(
        _router_kernel,
        grid=(n // tm,),
        in_specs=[pl.BlockSpec((tm, D_MODEL), lambda i: (i, 0)),
                  pl.BlockSpec((None, D_MODEL, LANES), lambda i: (i_moe, 0, 0)),
                  pl.BlockSpec((None, 1, LANES), lambda i: (i_moe, 0, 0))],
        out_specs=pl.BlockSpec((N_EXPERTS, tm, LANES), lambda i: (0, i, 0)),
        out_shape=jax.ShapeDtypeStruct((N_EXPERTS, n, LANES), F32),
        compiler_params=_cparams(1), name="moe_router",
    )(x, w["wr"], w["br"])


def _moe_kernel(x_ref, gate_ref, wg_ref, wu_ref, wd_ref, g_ref, b_ref, out_ref, acc_ref):
    e = pl.program_id(1)
    f = pl.program_id(2)

    @pl.when((e == 0) & (f == 0))
    def _():
        acc_ref[...] = jnp.zeros(acc_ref.shape, F32)

    x = x_ref[...].astype(BF16)
    hid = jax.nn.silu(_dot(x, wg_ref[...])) * _dot(x, wu_ref[...])
    y = _dot(hid.astype(BF16), wd_ref[...])
    gate = jnp.concatenate([gate_ref[...]] * (D_MODEL // LANES), axis=1)
    acc_ref[...] += gate * y

    @pl.when((e == pl.num_programs(1) - 1) & (f == pl.num_programs(2) - 1))
    def _():
        out_ref[...] = _layer_norm(ALPHA * x_ref[...] + acc_ref[...], g_ref[...], b_ref[...])


def _ffn_moe(x, gates, w, i_moe, l):
    n = x.shape[0]
    tm = TOKEN_BLOCK
    d_ff = w["wg_e"].shape[-1]
    tf = _ff_chunk(d_ff)
    return pl.pallas_call(
        _moe_kernel,
        grid=(n // tm, N_EXPERTS, d_ff // tf),
        in_specs=[pl.BlockSpec((tm, D_MODEL), lambda i, e, f: (i, 0)),
                  pl.BlockSpec((None, tm, LANES), lambda i, e, f: (e, i, 0)),
                  pl.BlockSpec((None, None, D_MODEL, tf), lambda i, e, f: (i_moe, e, 0, f)),
                  pl.BlockSpec((None, None, D_MODEL, tf), lambda i, e, f: (i_moe, e, 0, f)),
                  pl.BlockSpec((None, None, tf, D_MODEL), lambda i, e, f: (i_moe, e, f, 0)),
                  pl.BlockSpec((None, 1, D_MODEL), lambda i, e, f: (l, 0, 0)),
                  pl.BlockSpec((None, 1, D_MODEL), lambda i, e, f: (l, 0, 0))],
        out_specs=pl.BlockSpec((tm, D_MODEL), lambda i, e, f: (i, 0)),
        out_shape=jax.ShapeDtypeStruct((n, D_MODEL), F32),
        scratch_shapes=[pltpu.VMEM((tm, D_MODEL), F32)],
        compiler_params=_cparams(3), name="ffn_moe",
    )(x, gates, w["wg_e"], w["wu_e"], w["wd_e"], w["ln2g"], w["ln2b"])


def _swap_halves(w):
    half = w.shape[-1] // 2
    return jnp.concatenate([w[..., half:], w[..., :half]], axis=-1)


def _prepare_weights(w_in, idx_ln_g, idx_ln_b, q_norm_g, w_uq, kv_norm_g, w_uk, w_uv, w_out, ln1_g, ln1_b,
                     ln2_g, ln2_b, w_gate_dense, w_up_dense, w_down_dense, w_router, b_router,
                     w_gate_exp, w_up_exp, w_down_exp):
    bf = lambda a: a.astype(BF16)
    depth = w_in.shape[0]
    splits = (H_A * DH_A, DH_A, DH_A, H_IDX * D_IDX, D_IDX, H_IDX, Q_LORA, KV_LORA, D_ROPE)
    cols, start = [], 0
    for width in splits:
        cols.append(w_in[:, :, start:start + width])
        start += width
    w_qa, w_ka, w_va, w_qidx, w_kidx, w_widx, w_cq, w_ckv, w_kr = cols
    w_uq4 = w_uq.reshape(depth, Q_LORA, H_B, D_NOPE + D_ROPE)
    w_uq_rope = w_uq4[..., D_NOPE:]
    vec = lambda a: a[:, None, :]
    return {
        "wqa": bf(w_qa),
        "wkva": bf(jnp.concatenate([w_ka, w_va], axis=-1)),
        "wqidx": bf(w_qidx.reshape(depth, D_MODEL, H_IDX, D_IDX).transpose(0, 2, 1, 3)),
        "wkidx": bf(w_kidx),
        "wwidx": bf(jnp.pad(w_widx, ((0, 0), (0, 0), (0, LANES - H_IDX)))),
        "wcq": bf(w_cq),
        "wckv": bf(w_ckv),
        "wkr": bf(jnp.stack([w_kr, _swap_halves(w_kr)], axis=1)),
        "idxg": vec(idx_ln_g), "idxb": vec(idx_ln_b), "qng": vec(q_norm_g), "kvng": vec(kv_norm_g),
        "wuqn": bf(w_uq4[..., :D_NOPE].transpose(0, 2, 1, 3)),
        "wuqr": bf(w_uq_rope.transpose(0, 2, 1, 3)),
        "wuqrs": bf(_swap_halves(w_uq_rope).transpose(0, 2, 1, 3)),
        "wukt": bf(w_uk.transpose(0, 2, 3, 1)),
        "wuv": bf(w_uv.transpose(0, 2, 1, 3)),
        "wout": bf(w_out),
        "ln1g": vec(ln1_g), "ln1b": vec(ln1_b), "ln2g": vec(ln2_g), "ln2b": vec(ln2_b),
        "wg_d": bf(w_gate_dense), "wu_d": bf(w_up_dense), "wd_d": bf(w_down_dense),
        "wr": jnp.pad(w_router, ((0, 0), (0, 0), (0, LANES - N_EXPERTS))),
        "br": jnp.pad(b_router, ((0, 0), (0, LANES - N_EXPERTS)))[:, None, :],
        "wg_e": bf(w_gate_exp), "wu_e": bf(w_up_exp), "wd_e": bf(w_down_exp),
    }


def _rope_tables(pos):
    half = D_ROPE // 2
    inv = ROPE_THETA ** (-jnp.arange(half, dtype=F32) / half)
    ang = pos.astype(F32)[:, None] * inv[None, :]
    cos, sin = jnp.cos(ang), jnp.sin(ang)
    return jnp.concatenate([cos, cos], axis=-1), jnp.concatenate([-sin, sin], axis=-1)


def _pad_new(a, db, t_new):
    a = a.reshape(db, t_new, a.shape[-1])
    return jnp.pad(a, ((0, 0), (0, LANES - t_new), (0, 0)))


def kernel(x_prompt, x_sample, cache_a_kv, cache_a_kidx, cache_b_latent, page_table, rel_bias, w_in, idx_ln_g, idx_ln_b, q_norm_g, w_uq, kv_norm_g, w_uk, w_uv, w_out, ln1_g, ln1_b, ln2_g, ln2_b, w_gate_dense, w_up_dense, w_down_dense, w_router, b_router, w_gate_exp, w_up_exp, w_down_exp):
    batch, seq, _ = x_prompt.shape
    db, t_new, _ = x_sample.shape
    depth = w_in.shape[0]
    n_prompt = batch * seq
    n_sample = db * t_new
    n = n_prompt + n_sample
    past_len = page_table.shape[1] * PAGE_SIZE
    assert seq % KEY_CHUNK == 0 and n % TOKEN_BLOCK == 0 and n_prompt % t_new == 0 and t_new % 8 == 0
    assert page_table.shape[1] % min(PAGES_PER_STEP, page_table.shape[1]) == 0

    w = _prepare_weights(w_in, idx_ln_g, idx_ln_b, q_norm_g, w_uq, kv_norm_g, w_uk, w_uv, w_out, ln1_g, ln1_b,
                         ln2_g, ln2_b, w_gate_dense, w_up_dense, w_down_dense, w_router, b_router,
                         w_gate_exp, w_up_exp, w_down_exp)
    pos = jnp.concatenate([jnp.tile(jnp.arange(seq, dtype=I32), batch),
                           jnp.tile(past_len + jnp.arange(t_new, dtype=I32), db)])
    cos_t, sin_t = _rope_tables(pos)
    bias_tile, bias_past, bias_new = _bias_tables(rel_bias, past_len, t_new)

    x = jnp.concatenate([x_prompt.reshape(n_prompt, D_MODEL), x_sample.reshape(n_sample, D_MODEL)], axis=0)
    kva_all, kidx_all, kvb_all = [], [], []
    for l in range(depth):
        qa, kva, kva_bf, qidx, kidx, kidx_bf, widx, qcat, kvb, kvb_bf = _projections(x, cos_t, sin_t, w, l)
        a_p = _dsa_prompt(qa, qidx, widx, kidx_bf, kva_bf, bias_tile, batch, seq)
        o_p = _mla_prompt(qcat, kvb_bf, batch, seq)
        qidx_s = qidx[:, n_prompt:].astype(F32)
        qa_s = qa[n_prompt:].astype(F32)
        qcat_s = qcat[:, n_prompt:].astype(F32)
        sel = _samp_idx(page_table, qidx_s, widx, _pad_new(kidx_bf[n_prompt:], db, t_new), cache_a_kidx, l,
                        n_prompt, t_new)
        a_s = _samp_dsa(page_table, qa_s, sel, _pad_new(kva_bf[n_prompt:], db, t_new), bias_past, bias_new,
                        cache_a_kv, l, t_new)
        o_s = _samp_mla(page_table, qcat_s, _pad_new(kvb_bf[n_prompt:], db, t_new), cache_b_latent, l, t_new)
        a = jnp.concatenate([a_p, a_s], axis=0)
        o_lat = jnp.concatenate([o_p, o_s], axis=1)
        x = _merge(x, a, o_lat, w, l)
        if l % 2 == 0:
            x = _ffn_dense(x, w, l // 2, l)
        else:
            x = _ffn_moe(x, _router(x, w, l // 2), w, l // 2, l)
        kva_all.append(kva)
        kidx_all.append(kidx)
        kvb_all.append(kvb)

    def split(parts, width):
        full = jnp.stack(parts)
        return (full[:, :n_prompt].reshape(depth, batch, seq, width),
                full[:, n_prompt:].reshape(depth, db, t_new, width))

    kva_p, kva_s = split(kva_all, C_A)
    kidx_p, kidx_s = split(kidx_all, D_IDX)
    kvb_p, kvb_s = split(kvb_all, C_B)
    return (x[:n_prompt].reshape(batch, seq, D_MODEL), x[n_prompt:].reshape(db, t_new, D_MODEL),
            kva_p, kidx_p, kvb_p, kva_s, kidx_s, kvb_s)
```

```python
import functools
import math

import jax
import jax.numpy as jnp
from jax import lax
from jax.experimental import pallas as pl
from jax.experimental.pallas import tpu as pltpu

F32 = jnp.float32
BF16 = jnp.bfloat16
I32 = jnp.int32

D_MODEL = 1024
DEPTH = 4
PAGE_SIZE = 128
H_A = 4
DH_A = 128
H_IDX = 8
D_IDX = 64
TOPK_MAX = 256
N_BUCKETS = 32
T5_MAX_DIST = 128
H_B = 4
D_NOPE = 128
D_ROPE = 64
D_V = 128
Q_LORA = 384
KV_LORA = 256
ROPE_THETA = 10000.0
C_A = 2 * DH_A
C_B = KV_LORA + D_ROPE
N_EXPERTS = 8
ALPHA = (2 * DEPTH) ** 0.25
EPS = 1e-5
Q_BLOCK = 128
MLA_SCALE = (D_NOPE + D_ROPE) ** -0.5
LOG2E = math.log2(math.e)
QA_SCALE = DH_A ** -0.5 * LOG2E
QCAT_SCALE = MLA_SCALE * LOG2E

LANES = 128
TOKEN_BLOCK = 512
KEY_CHUNK = 512
SCORE_TILE = 256
BIAS_TILE = 256
MLA_KEY_CHUNK = 512
PAGES_PER_STEP = 16
VMEM_LIMIT = 56 * 1024 * 1024

assert 3 * Q_BLOCK - (BIAS_TILE - 1) >= T5_MAX_DIST and KEY_CHUNK % BIAS_TILE == 0 and KEY_CHUNK % SCORE_TILE == 0

INT_MIN = -(2 ** 31)
NEG_BIG = -1e30
NT_DIMS = (((1,), (1,)), ((), ()))


def _cparams(n_axes):
    return pltpu.CompilerParams(dimension_semantics=("arbitrary",) * n_axes,
                                vmem_limit_bytes=VMEM_LIMIT)


def _dot(a, b):
    return jnp.dot(a, b, preferred_element_type=F32)


def _dot_nt(a, b):
    return lax.dot_general(a, b, NT_DIMS, preferred_element_type=F32)


def _layer_norm(x, g, b):
    xc = x - jnp.mean(x, axis=-1, keepdims=True)
    y = xc * lax.rsqrt(jnp.mean(xc * xc, axis=-1, keepdims=True) + EPS)
    return y * g + b


def _rms_norm(x, g):
    return x * lax.rsqrt(jnp.mean(x * x, axis=-1, keepdims=True) + EPS) * g


def _order_key(score):
    bits = lax.bitcast_convert_type(score, I32)
    return jnp.where(bits < 0, bits ^ jnp.int32(0x7FFFFFFF), bits)


def _bias_of_dist(dist, relb_ref, h):
    n = jnp.maximum(dist, 0)
    max_exact = N_BUCKETS // 2
    nf = jnp.maximum(n, max_exact).astype(F32)
    large = max_exact + (jnp.log(nf / max_exact) / math.log(T5_MAX_DIST / max_exact)
                         * (N_BUCKETS - max_exact)).astype(I32)
    bucket = jnp.where(n < max_exact, n, jnp.minimum(large, N_BUCKETS - 1))
    out = jnp.zeros(dist.shape, F32)
    for b in range(N_BUCKETS):
        out = jnp.where(bucket == b, relb_ref[b, h], out)
    return out * LOG2E


def _bias_tables_kernel(relb_ref, tile_ref, past_ref, new_ref, *, past_len, t_new):
    r = lax.broadcasted_iota(I32, (Q_BLOCK, BIAS_TILE), 0)
    c = lax.broadcasted_iota(I32, (Q_BLOCK, BIAS_TILE), 1)
    for m in range(4):
        for h in range(H_A):
            tile_ref[m, h] = _bias_of_dist(Q_BLOCK * m + r - c, relb_ref, h) - relb_ref[N_BUCKETS - 1, h] * LOG2E
    q = lax.broadcasted_iota(I32, (t_new, past_len), 0)
    s = lax.broadcasted_iota(I32, (t_new, past_len), 1)
    qn = lax.broadcasted_iota(I32, (t_new, LANES), 0)
    sn = lax.broadcasted_iota(I32, (t_new, LANES), 1)
    for h in range(H_A):
        past_ref[h * t_new:(h + 1) * t_new, :] = _bias_of_dist(past_len + q - s, relb_ref, h)
        new_ref[h * t_new:(h + 1) * t_new, :] = _bias_of_dist(qn - sn, relb_ref, h)


def _bias_tables(rel_bias, past_len, t_new):
    return pl.pallas_call(
        functools.partial(_bias_tables_kernel, past_len=past_len, t_new=t_new),
        out_shape=(jax.ShapeDtypeStruct((4, H_A, Q_BLOCK, BIAS_TILE), F32),
                   jax.ShapeDtypeStruct((H_A * t_new, past_len), F32),
                   jax.ShapeDtypeStruct((H_A * t_new, LANES), F32)),
        in_specs=[pl.BlockSpec(memory_space=pltpu.SMEM)],
        name="bias_tables",
    )(rel_bias)


def _proj_kernel(x_ref, cos_ref, sin_ref, wqa, wkva, wqidx, wkidx, wwidx, wcq, wckv, wkr,
                 idxg, idxb, qng, kvng, wuqn, wuqr, wuqrs, wukt,
                 qa_o, kva_o, kvabf_o, qidx_o, kidx_o, kidxbf_o, widx_o, qcat_o, kvb_o, kvbbf_o):
    x = x_ref[...].astype(BF16)
    cos = cos_ref[...]
    sin = sin_ref[...]
    qa_o[...] = (_dot(x, wqa[...]) * QA_SCALE).astype(BF16)
    kva = _dot(x, wkva[...])
    kva_o[...] = kva
    kvabf_o[...] = kva.astype(BF16)
    for h in range(H_IDX):
        qidx_o[h] = (_dot(x, wqidx[h]) * (D_IDX ** -0.5)).astype(BF16)
    kidx = _layer_norm(_dot(x, wkidx[...]), idxg[...], idxb[...])
    kidx_o[...] = kidx
    kidxbf_o[...] = kidx.astype(BF16)
    widx_o[...] = _dot(x, wwidx[...])[:, :H_IDX] * (H_IDX ** -0.5)
    cq = _rms_norm(_dot(x, wcq[...]), qng[...]).astype(BF16)
    for h in range(H_B):
        q_nope = _dot(cq, wuqn[h]).astype(BF16)
        q_lat = _dot(q_nope, wukt[h])
        q_rope = _dot(cq, wuqr[h]) * cos + _dot(cq, wuqrs[h]) * sin
        qcat_o[h, :, 0:KV_LORA] = (q_lat * QCAT_SCALE).astype(BF16)
        qcat_o[h, :, KV_LORA:C_B] = (q_rope * QCAT_SCALE).astype(BF16)
    ckv = _rms_norm(_dot(x, wckv[...]), kvng[...])
    k_rope = _dot(x, wkr[0]) * cos + _dot(x, wkr[1]) * sin
    kvb_o[:, 0:KV_LORA] = ckv
    kvb_o[:, KV_LORA:C_B] = k_rope
    kvbbf_o[:, 0:KV_LORA] = ckv.astype(BF16)
    kvbbf_o[:, KV_LORA:C_B] = k_rope.astype(BF16)


def _projections(x, cos_t, sin_t, w, l):
    n = x.shape[0]
    tm = TOKEN_BLOCK
    row = lambda width: pl.BlockSpec((tm, width), lambda i: (i, 0))
    lay2 = lambda a: pl.BlockSpec((None,) + a.shape[1:], lambda i: (l,) + (0,) * (a.ndim - 1))
    weights = [w["wqa"], w["wkva"], w["wqidx"], w["wkidx"], w["wwidx"], w["wcq"], w["wckv"], w["wkr"],
               w["idxg"], w["idxb"], w["qng"], w["kvng"], w["wuqn"], w["wuqr"], w["wuqrs"], w["wukt"]]
    out_shape = (
        jax.ShapeDtypeStruct((n, H_A * DH_A), BF16),
        jax.ShapeDtypeStruct((n, C_A), F32),
        jax.ShapeDtypeStruct((n, C_A), BF16),
        jax.ShapeDtypeStruct((H_IDX, n, D_IDX), BF16),
        jax.ShapeDtypeStruct((n, D_IDX), F32),
        jax.ShapeDtypeStruct((n, D_IDX), BF16),
        jax.ShapeDtypeStruct((n, H_IDX), F32),
        jax.ShapeDtypeStruct((H_B, n, C_B), BF16),
        jax.ShapeDtypeStruct((n, C_B), F32),
        jax.ShapeDtypeStruct((n, C_B), BF16),
    )
    out_specs = (row(H_A * DH_A), row(C_A), row(C_A),
                 pl.BlockSpec((H_IDX, tm, D_IDX), lambda i: (0, i, 0)),
                 row(D_IDX), row(D_IDX), row(H_IDX),
                 pl.BlockSpec((H_B, tm, C_B), lambda i: (0, i, 0)),
                 row(C_B), row(C_B))
    return pl.pallas_call(
        _proj_kernel,
        grid=(n // tm,),
        in_specs=[row(D_MODEL), row(D_ROPE), row(D_ROPE)] + [lay2(a) for a in weights],
        out_specs=out_specs,
        out_shape=out_shape,
        compiler_params=_cparams(1),
        name="projections",
    )(x, cos_t, sin_t, *weights)


def _lane_tile(x, width):
    reps = width // LANES
    return x if reps == 1 else jnp.concatenate([x] * reps, axis=1)


def _softmax_init(m_ref, l_ref, acc_ref):
    m_ref[...] = jnp.full(m_ref.shape, NEG_BIG, F32)
    l_ref[...] = jnp.zeros(l_ref.shape, F32)
    acc_ref[...] = jnp.zeros(acc_ref.shape, F32)


def _softmax_step(s, v, m_ref, l_ref, acc_ref, v_transposed=False):
    m_old = m_ref[...]
    m_new = jnp.maximum(m_old, jnp.max(s, axis=1, keepdims=True))
    p = jnp.exp2(s - _lane_tile(m_new, s.shape[1]))
    alpha = jnp.exp2(m_old - m_new)
    l_ref[...] = alpha * l_ref[...] + jnp.sum(p, axis=1, keepdims=True)
    pv = _dot_nt(p.astype(BF16), v) if v_transposed else _dot(p.astype(BF16), v)
    acc_ref[...] = _lane_tile(alpha, acc_ref.shape[1]) * acc_ref[...] + pv
    m_ref[...] = m_new


def _select_threshold(keys_ref, n_rows, n_chunks, chunk, n_keep, col_bits, bracket=None):
    whole = n_chunks is None
    lane_iota = lax.broadcasted_iota(I32, (n_rows, LANES), 1)

    def count(pred):
        if whole:
            ind = [jnp.where(pred(keys_ref[:, g * LANES:(g + 1) * LANES], g * LANES), 1.0, 0.0)
                   for g in range(chunk // LANES)]
            while len(ind) > 1:
                ind = [a + b for a, b in zip(ind[0::2], ind[1::2])] + ([ind[-1]] if len(ind) % 2 else [])
            return jnp.sum(ind[0], axis=1, keepdims=True)

        def body(j, c):
            off = pl.multiple_of(j * chunk, chunk)
            for g in range(chunk // LANES):
                col0 = off + g * LANES
                c = c + jnp.where(pred(keys_ref[:, pl.ds(pl.multiple_of(col0, LANES), LANES)], col0), 1.0, 0.0)
            return c
        c = lax.fori_loop(0, n_chunks, body, jnp.zeros((n_rows, LANES), F32))
        return jnp.sum(c, axis=1, keepdims=True)

    def lanes(col):
        return jnp.broadcast_to(col, (n_rows, LANES))

    if bracket is None:
        def bit_body(p, prefix):
            cand = prefix + lax.shift_left(jnp.int32(1), 31 - p)
            cand_b = lanes(cand)
            cnt = count(lambda t, col0: t >= cand_b)
            return jnp.where(cnt >= n_keep, cand, prefix)

        thr = lax.fori_loop(0, 32, bit_body, jnp.full((n_rows, 1), INT_MIN, I32))
    else:
        def probe(state):
            lo, hi, thr, done, _ = state
            mid = (lo >> 1) + (hi >> 1) + (lo & hi & 1)
            mid_b = lanes(mid)
            cnt = count(lambda t, col0: t >= mid_b)
            enough = cnt >= n_keep
            exact = cnt == n_keep
            thr = jnp.where(exact & (done == 0), mid, thr)
            done = jnp.where(exact, 1, done)
            lo = jnp.where(enough, mid, lo)
            hi = jnp.where(enough, hi, mid)
            still = jnp.max(jnp.where((done == 0) & (hi > lo + 1), 1.0, 0.0))
            return lo, hi, thr, done, still

        lo0, hi0 = bracket
        state = (lo0, hi0, lo0, jnp.zeros((n_rows, 1), I32), jnp.max(jnp.where(hi0 > lo0 + 1, 1.0, 0.0)))
        lo, _, thr, done, _ = lax.while_loop(lambda state: state[4] > 0.0, probe, state)
        thr = jnp.where(done == 1, thr, lo)
    thr = jnp.maximum(thr, INT_MIN + 1)
    thr_b = lanes(thr)
    n_gt = count(lambda t, col0: t > thr_b)
    n_eq = count(lambda t, col0: t == thr_b)
    need = n_keep - n_gt
    tie = jnp.max(jnp.where(n_eq > need, 1.0, 0.0)) > 0.0

    @pl.when(tie)
    def _():
        def cut_body(p, cut):
            cand = cut + lax.shift_left(jnp.int32(1), col_bits - 1 - p)
            cand_b = lanes(cand)
            cnt = count(lambda t, col0: (t == thr_b) & (col0 + lane_iota < cand_b))
            return jnp.where(cnt < need, cand, cut)

        cut_b = lanes(lax.fori_loop(0, col_bits, cut_body, jnp.zeros((n_rows, 1), I32)))

        def demote_tile(col0):
            t = keys_ref[:, pl.ds(col0, LANES)]
            lose = (t == thr_b) & (col0 + lane_iota > cut_b)
            keys_ref[:, pl.ds(col0, LANES)] = jnp.where(lose, thr_b - 1, t)

        if whole:
            for g in range(chunk // LANES):
                demote_tile(g * LANES)
        else:
            def demote(j, carry):
                for g in range(chunk // LANES):
                    demote_tile(pl.multiple_of(j * chunk + g * LANES, LANES))
                return carry
            lax.fori_loop(0, n_chunks, demote, 0)

    return thr


def _dsa_prompt_kernel(qidx_ref, widx_ref, qa_ref, kidx_ref, kva_ref, bias_ref, out_ref,
                       keys_ref, wb_ref, gmax_ref, qs_ref, m_ref, l_ref, acc_ref, *, n_keep, col_bits):
    i = pl.program_id(1)
    kc = KEY_CHUNK
    st = SCORE_TILE
    n_chunks = i // (kc // Q_BLOCK) + 1
    row_pos = i * Q_BLOCK + lax.broadcasted_iota(I32, (Q_BLOCK, st), 0)
    lane = lax.broadcasted_iota(I32, (Q_BLOCK, st), 1)

    for h in range(H_IDX):
        wb_ref[h] = jnp.broadcast_to(widx_ref[:, h:h + 1], (Q_BLOCK, st))
    gmax_ref[...] = jnp.full(gmax_ref.shape, -jnp.inf, F32)

    def score_body(j, carry):
        for sub in range(kc // st):
            off = pl.multiple_of(j * kc + sub * st, st)
            keys_c = kidx_ref[pl.ds(off, st), :]
            score = jnp.zeros((Q_BLOCK, st), F32)
            for h in range(H_IDX):
                score = score + wb_ref[h] * jnp.maximum(_dot_nt(qidx_ref[h], keys_c), 0.0)
            visible = off + lane <= row_pos
            keys_ref[:, pl.ds(off, st)] = jnp.where(visible, _order_key(score), INT_MIN)
            gmax_ref[...] = jnp.maximum(gmax_ref[...], jnp.where(visible, score, -jnp.inf))
        return carry

    lax.fori_loop(0, n_chunks, score_body, 0)
    gmax = gmax_ref[...]
    lo0 = _order_key(jnp.min(gmax, axis=1, keepdims=True))
    hi0 = _order_key(jnp.max(gmax, axis=1, keepdims=True)) + 1
    thr = _select_threshold(keys_ref, Q_BLOCK, n_chunks, kc, n_keep, col_bits, bracket=(lo0, hi0))

    rows = H_A * Q_BLOCK
    _softmax_init(m_ref, l_ref, acc_ref)
    for h in range(H_A):
        qs_ref[h * Q_BLOCK:(h + 1) * Q_BLOCK, :] = qa_ref[:, h * DH_A:(h + 1) * DH_A]
    thr_b = jnp.broadcast_to(thr, (Q_BLOCK, kc))
    tiles_per_chunk = kc // BIAS_TILE
    n_far = jnp.maximum(n_chunks - 2, 0)

    def logits(j):
        return _dot_nt(qs_ref[...], kva_ref[pl.ds(pl.multiple_of(j * kc, kc), kc), 0:DH_A])

    def attend(j, s_raw, near):
        s_next = logits(jnp.minimum(j + 1, n_chunks - 1))
        off = pl.multiple_of(j * kc, kc)
        add = jnp.where(keys_ref[:, pl.ds(off, kc)] >= thr_b, 0.0, NEG_BIG)[None]
        if near:
            m = i - (kc // Q_BLOCK) * j
            tiles = [bias_ref[jnp.clip(m - (BIAS_TILE // Q_BLOCK) * t, 0, 3)] for t in range(tiles_per_chunk)]
            add = add + jnp.concatenate(tiles, axis=2)
        s = (s_raw.reshape(H_A, Q_BLOCK, kc) + add).reshape(rows, kc)
        _softmax_step(s, kva_ref[pl.ds(off, kc), DH_A:C_A], m_ref, l_ref, acc_ref)
        return s_next

    s_raw = lax.fori_loop(0, n_far, lambda j, s: attend(j, s, False), logits(0))
    lax.fori_loop(n_far, n_chunks, lambda j, s: attend(j, s, True), s_raw)
    o = acc_ref[...] / l_ref[...]
    for h in range(H_A):
        out_ref[:, h * DH_A:(h + 1) * DH_A] = o[h * Q_BLOCK:(h + 1) * Q_BLOCK, :]


def _dsa_prompt(qa, qidx, widx, kidx_bf, kva_bf, bias_tile, batch, seq):
    nq = seq // Q_BLOCK
    n_keep = min(TOPK_MAX, seq // 4)
    assert n_keep <= SCORE_TILE and DH_A == LANES
    kern = functools.partial(_dsa_prompt_kernel, n_keep=n_keep, col_bits=max(1, (seq - 1).bit_length()))
    return pl.pallas_call(
        kern,
        grid=(batch, nq),
        in_specs=[
            pl.BlockSpec((H_IDX, Q_BLOCK, D_IDX), lambda b, i: (0, b * nq + i, 0)),
            pl.BlockSpec((Q_BLOCK, H_IDX), lambda b, i: (b * nq + i, 0)),
            pl.BlockSpec((Q_BLOCK, H_A * DH_A), lambda b, i: (b * nq + i, 0)),
            pl.BlockSpec((seq, D_IDX), lambda b, i: (b, 0)),
            pl.BlockSpec((seq, C_A), lambda b, i: (b, 0)),
            pl.BlockSpec(bias_tile.shape, lambda b, i: (0, 0, 0, 0)),
        ],
        out_specs=pl.BlockSpec((Q_BLOCK, H_A * DH_A), lambda b, i: (b * nq + i, 0)),
        out_shape=jax.ShapeDtypeStruct((batch * seq, H_A * DH_A), F32),
        scratch_shapes=[
            pltpu.VMEM((Q_BLOCK, seq), I32),
            pltpu.VMEM((H_IDX, Q_BLOCK, SCORE_TILE), F32),
            pltpu.VMEM((Q_BLOCK, SCORE_TILE), F32),
            pltpu.VMEM((H_A * Q_BLOCK, DH_A), BF16),
            pltpu.VMEM((H_A * Q_BLOCK, LANES), F32),
            pltpu.VMEM((H_A * Q_BLOCK, LANES), F32),
            pltpu.VMEM((H_A * Q_BLOCK, DH_A), F32),
        ],
        compiler_params=_cparams(2),
        name="dsa_prompt",
    )(qidx, widx, qa, kidx_bf, kva_bf, bias_tile)


def _mla_prompt_kernel(q_ref, kvb_ref, out_ref, m_ref, l_ref, acc_ref):
    i = pl.program_id(1)
    kc = MLA_KEY_CHUNK
    rows = H_B * Q_BLOCK
    n_full = i // (kc // Q_BLOCK)
    q = q_ref[...].reshape(rows, C_B)
    _softmax_init(m_ref, l_ref, acc_ref)

    def keys(j):
        return kvb_ref[pl.ds(pl.multiple_of(j * kc, kc), kc), :]

    def body(j, s):
        s_next = _dot_nt(q, keys(j + 1))
        _softmax_step(s, keys(j)[:, :KV_LORA], m_ref, l_ref, acc_ref)
        return s_next

    s_last = lax.fori_loop(0, n_full, body, _dot_nt(q, keys(0)))
    row_pos = i * Q_BLOCK + lax.broadcasted_iota(I32, (H_B, Q_BLOCK, kc), 1).reshape(rows, kc)
    visible = n_full * kc + lax.broadcasted_iota(I32, (rows, kc), 1) <= row_pos
    s_last = jnp.where(visible, s_last, NEG_BIG)
    _softmax_step(s_last, keys(n_full)[:, :KV_LORA], m_ref, l_ref, acc_ref)
    out_ref[...] = (acc_ref[...] / _lane_tile(l_ref[...], KV_LORA)).reshape(H_B, Q_BLOCK, KV_LORA)


def _mla_prompt(qcat, kvb_bf, batch, seq):
    nq = seq // Q_BLOCK
    rows = H_B * Q_BLOCK
    return pl.pallas_call(
        _mla_prompt_kernel,
        grid=(batch, nq),
        in_specs=[
            pl.BlockSpec((H_B, Q_BLOCK, C_B), lambda b, i: (0, b * nq + i, 0)),
            pl.BlockSpec((seq, C_B), lambda b, i: (b, 0)),
        ],
        out_specs=pl.BlockSpec((H_B, Q_BLOCK, KV_LORA), lambda b, i: (0, b * nq + i, 0)),
        out_shape=jax.ShapeDtypeStruct((H_B, batch * seq, KV_LORA), F32),
        scratch_shapes=[
            pltpu.VMEM((rows, LANES), F32),
            pltpu.VMEM((rows, LANES), F32),
            pltpu.VMEM((rows, KV_LORA), F32),
        ],
        compiler_params=_cparams(2),
        name="mla_prompt",
    )(qcat, kvb_bf)


def _page_specs(l, page_shape, pps):
    def spec(p):
        return pl.BlockSpec((None, None) + page_shape,
                            lambda b, c, pt: (l, pt[b, c * pps + p], 0, 0))
    return [spec(p) for p in range(pps)]


def _samp_idx_kernel(pt_ref, qidx_ref, widx_ref, knew_ref, *rest, pps, past_len, t_new, n_keep, col_bits):
    pages = rest[:pps]
    sel_ref, kbuf, keys_ref = rest[pps:]
    c = pl.program_id(1)
    n_steps = pl.num_programs(1)
    total = past_len + LANES
    span = pps * PAGE_SIZE
    q = qidx_ref[...].reshape(H_IDX * t_new, D_IDX).astype(BF16)

    def weighted(r):
        score = jnp.zeros((t_new, r.shape[1]), F32)
        for h in range(H_IDX):
            score = score + jnp.broadcast_to(widx_ref[:, h:h + 1], (t_new, r.shape[1])) * r[h * t_new:(h + 1) * t_new, :]
        return score

    for p in range(pps):
        kbuf[:, p * PAGE_SIZE:(p + 1) * PAGE_SIZE] = pages[p][...].astype(BF16)
    off = pl.multiple_of(c * span, span)
    keys_ref[:, pl.ds(off, span)] = _order_key(weighted(jnp.maximum(_dot(q, kbuf[...]), 0.0)))

    @pl.when(c == n_steps - 1)
    def _():
        qi = lax.broadcasted_iota(I32, (t_new, LANES), 0)
        ki = lax.broadcasted_iota(I32, (t_new, LANES), 1)
        score_new = weighted(jnp.maximum(_dot_nt(q, knew_ref[...]), 0.0))
        keys_ref[:, past_len:total] = jnp.where(ki <= qi, _order_key(score_new), INT_MIN)
        thr = _select_threshold(keys_ref, t_new, None, total, n_keep, col_bits)
        sel_ref[...] = jnp.where(keys_ref[...] >= thr, 1.0, 0.0)


def _samp_idx(page_table, qidx, widx, kidx_new_pad, cache, l, n_prompt, t_new):
    db, n_pages = page_table.shape
    pps = min(PAGES_PER_STEP, n_pages)
    past_len = n_pages * PAGE_SIZE
    total = past_len + LANES
    n_keep = min(TOPK_MAX, (past_len + t_new) // 4)
    base = n_prompt // t_new
    kern = functools.partial(_samp_idx_kernel, pps=pps, past_len=past_len, t_new=t_new, n_keep=n_keep,
                             col_bits=max(1, (total - 1).bit_length()))
    grid_spec = pltpu.PrefetchScalarGridSpec(
        num_scalar_prefetch=1,
        grid=(db, n_pages // pps),
        in_specs=[
            pl.BlockSpec((H_IDX, t_new, D_IDX), lambda b, c, pt: (0, b, 0)),
            pl.BlockSpec((t_new, H_IDX), lambda b, c, pt: (base + b, 0)),
            pl.BlockSpec((None, LANES, D_IDX), lambda b, c, pt: (b, 0, 0)),
        ] + _page_specs(l, (D_IDX, PAGE_SIZE), pps),
        out_specs=pl.BlockSpec((None, t_new, total), lambda b, c, pt: (b, 0, 0)),
        scratch_shapes=[pltpu.VMEM((D_IDX, pps * PAGE_SIZE), BF16),
                        pltpu.VMEM((t_new, total), I32)],
    )
    return pl.pallas_call(
        kern, grid_spec=grid_spec,
        out_shape=jax.ShapeDtypeStruct((db, t_new, total), F32),
        compiler_params=_cparams(2), name="sample_indexer",
    )(page_table, qidx, widx, kidx_new_pad, *([cache] * pps))


def _samp_dsa_kernel(pt_ref, qa_ref, sel_ref, kvnew_ref, biasp_ref, biasn_ref, *rest, pps, past_len, t_new):
    pages = rest[:pps]
    out_ref, kbuf, m_ref, l_ref, acc_ref = rest[pps:]
    c = pl.program_id(1)
    n_steps = pl.num_programs(1)
    span = pps * PAGE_SIZE

    @pl.when(c == 0)
    def _():
        _softmax_init(m_ref, l_ref, acc_ref)

    def mask_add(sel):
        return jnp.concatenate([jnp.where(sel > 0.5, 0.0, NEG_BIG)] * H_A, axis=0)

    qa = qa_ref[...]
    q = jnp.concatenate([qa[:, h * DH_A:(h + 1) * DH_A] for h in range(H_A)], axis=0).astype(BF16)
    for p in range(pps):
        kbuf[p * PAGE_SIZE:(p + 1) * PAGE_SIZE, :] = pages[p][...].astype(BF16)
    off = pl.multiple_of(c * span, span)
    s = _dot_nt(q, kbuf[:, :DH_A]) + (biasp_ref[:, pl.ds(off, span)] + mask_add(sel_ref[:, pl.ds(off, span)]))
    _softmax_step(s, kbuf[:, DH_A:], m_ref, l_ref, acc_ref)

    @pl.when(c == n_steps - 1)
    def _():
        kvn = kvnew_ref[...]
        sn = _dot_nt(q, kvn[:, :DH_A]) + (biasn_ref[...] + mask_add(sel_ref[:, past_len:past_len + LANES]))
        _softmax_step(sn, kvn[:, DH_A:], m_ref, l_ref, acc_ref)
        o = acc_ref[...] / l_ref[...]
        for h in range(H_A):
            out_ref[:, h * DH_A:(h + 1) * DH_A] = o[h * t_new:(h + 1) * t_new, :]


def _samp_dsa(page_table, qa, sel, kva_new_pad, bias_past, bias_new, cache, l, t_new):
    db, n_pages = page_table.shape
    pps = min(PAGES_PER_STEP, n_pages)
    past_len = n_pages * PAGE_SIZE
    total = past_len + LANES
    rows = H_A * t_new
    kern = functools.partial(_samp_dsa_kernel, pps=pps, past_len=past_len, t_new=t_new)
    grid_spec = pltpu.PrefetchScalarGridSpec(
        num_scalar_prefetch=1,
        grid=(db, n_pages // pps),
        in_specs=[
            pl.BlockSpec((t_new, H_A * DH_A), lambda b, c, pt: (b, 0)),
            pl.BlockSpec((None, t_new, total), lambda b, c, pt: (b, 0, 0)),
            pl.BlockSpec((None, LANES, C_A), lambda b, c, pt: (b, 0, 0)),
            pl.BlockSpec((rows, past_len), lambda b, c, pt: (0, 0)),
            pl.BlockSpec((rows, LANES), lambda b, c, pt: (0, 0)),
        ] + _page_specs(l, (PAGE_SIZE, C_A), pps),
        out_specs=pl.BlockSpec((t_new, H_A * DH_A), lambda b, c, pt: (b, 0)),
        scratch_shapes=[
            pltpu.VMEM((pps * PAGE_SIZE, C_A), BF16),
            pltpu.VMEM((rows, LANES), F32),
            pltpu.VMEM((rows, LANES), F32),
            pltpu.VMEM((rows, DH_A), F32),
        ],
    )
    return pl.pallas_call(
        kern, grid_spec=grid_spec,
        out_shape=jax.ShapeDtypeStruct((db * t_new, H_A * DH_A), F32),
        compiler_params=_cparams(2), name="sample_dsa",
    )(page_table, qa, sel, kva_new_pad, bias_past, bias_new, *([cache] * pps))


def _samp_mla_kernel(pt_ref, q_ref, kvnew_ref, *rest, pps, t_new):
    pages = rest[:pps]
    out_ref, kbuf, m_ref, l_ref, acc_ref = rest[pps:]
    c = pl.program_id(1)
    n_steps = pl.num_programs(1)
    rows = H_B * t_new

    @pl.when(c == 0)
    def _():
        _softmax_init(m_ref, l_ref, acc_ref)

    q = q_ref[...].reshape(rows, C_B).astype(BF16)
    for p in range(pps):
        kbuf[:, p * PAGE_SIZE:(p + 1) * PAGE_SIZE] = pages[p][...].astype(BF16)
    _softmax_step(_dot(q, kbuf[...]), kbuf[0:KV_LORA, :], m_ref, l_ref, acc_ref, v_transposed=True)

    @pl.when(c == n_steps - 1)
    def _():
        kvn = kvnew_ref[...]
        qi = lax.broadcasted_iota(I32, (H_B, t_new, LANES), 1).reshape(rows, LANES)
        ki = lax.broadcasted_iota(I32, (rows, LANES), 1)
        sn = jnp.where(ki <= qi, _dot_nt(q, kvn), NEG_BIG)
        _softmax_step(sn, kvn[:, :KV_LORA], m_ref, l_ref, acc_ref)
        out_ref[...] = (acc_ref[...] / _lane_tile(l_ref[...], KV_LORA)).reshape(H_B, t_new, KV_LORA)


def _samp_mla(page_table, qcat, kvb_new_pad, cache, l, t_new):
    db, n_pages = page_table.shape
    pps = min(PAGES_PER_STEP, n_pages)
    rows = H_B * t_new
    kern = functools.partial(_samp_mla_kernel, pps=pps, t_new=t_new)
    grid_spec = pltpu.PrefetchScalarGridSpec(
        num_scalar_prefetch=1,
        grid=(db, n_pages // pps),
        in_specs=[
            pl.BlockSpec((H_B, t_new, C_B), lambda b, c, pt: (0, b, 0)),
            pl.BlockSpec((None, LANES, C_B), lambda b, c, pt: (b, 0, 0)),
        ] + _page_specs(l, (C_B, PAGE_SIZE), pps),
        out_specs=pl.BlockSpec((H_B, t_new, KV_LORA), lambda b, c, pt: (0, b, 0)),
        scratch_shapes=[
            pltpu.VMEM((C_B, pps * PAGE_SIZE), BF16),
            pltpu.VMEM((rows, LANES), F32),
            pltpu.VMEM((rows, LANES), F32),
            pltpu.VMEM((rows, KV_LORA), F32),
        ],
    )
    return pl.pallas_call(
        kern, grid_spec=grid_spec,
        out_shape=jax.ShapeDtypeStruct((H_B, db * t_new, KV_LORA), F32),
        compiler_params=_cparams(2), name="sample_mla",
    )(page_table, qcat, kvb_new_pad, *([cache] * pps))


def _merge_kernel(x_ref, a_ref, o_ref, wuv_ref, wout_ref, g_ref, b_ref, out_ref):
    width_a = H_A * DH_A
    mix = _dot(a_ref[...].astype(BF16), wout_ref[0:width_a, :])
    for h in range(H_B):
        b_h = _dot(o_ref[h].astype(BF16), wuv_ref[h]).astype(BF16)
        mix = mix + _dot(b_h, wout_ref[width_a + h * D_V:width_a + (h + 1) * D_V, :])
    out_ref[...] = _layer_norm(ALPHA * x_ref[...] + mix, g_ref[...], b_ref[...])


def _merge(x, a, o_lat, w, l):
    n = x.shape[0]
    tm = TOKEN_BLOCK
    row = lambda width: pl.BlockSpec((tm, width), lambda i: (i, 0))
    lay = lambda arr: pl.BlockSpec((None,) + arr.shape[1:], lambda i: (l,) + (0,) * (arr.ndim - 1))
    return pl.pallas_call(
        _merge_kernel,
        grid=(n // tm,),
        in_specs=[row(D_MODEL), row(H_A * DH_A), pl.BlockSpec((H_B, tm, KV_LORA), lambda i: (0, i, 0)),
                  lay(w["wuv"]), lay(w["wout"]), lay(w["ln1g"]), lay(w["ln1b"])],
        out_specs=row(D_MODEL),
        out_shape=jax.ShapeDtypeStruct((n, D_MODEL), F32),
        compiler_params=_cparams(1), name="merge_heads",
    )(x, a, o_lat, w["wuv"], w["wout"], w["ln1g"], w["ln1b"])


def _ffn_kernel(x_ref, wg_ref, wu_ref, wd_ref, g_ref, b_ref, out_ref, acc_ref):
    f = pl.program_id(1)

    @pl.when(f == 0)
    def _():
        acc_ref[...] = jnp.zeros(acc_ref.shape, F32)

    x = x_ref[...].astype(BF16)
    hid = jax.nn.silu(_dot(x, wg_ref[...])) * _dot(x, wu_ref[...])
    acc_ref[...] += _dot(hid.astype(BF16), wd_ref[...])

    @pl.when(f == pl.num_programs(1) - 1)
    def _():
        out_ref[...] = _layer_norm(ALPHA * x_ref[...] + acc_ref[...], g_ref[...], b_ref[...])


def _ff_chunk(d_ff):
    for parts in (2, 1, 11, 22):
        if d_ff % parts == 0 and (d_ff // parts) % LANES == 0:
            return d_ff // parts
    return d_ff


def _ffn_dense(x, w, i_dense, l):
    n = x.shape[0]
    tm = TOKEN_BLOCK
    d_ff = w["wg_d"].shape[-1]
    tf = _ff_chunk(d_ff)
    return pl.pallas_call(
        _ffn_kernel,
        grid=(n // tm, d_ff // tf),
        in_specs=[pl.BlockSpec((tm, D_MODEL), lambda i, f: (i, 0)),
                  pl.BlockSpec((None, D_MODEL, tf), lambda i, f: (i_dense, 0, f)),
                  pl.BlockSpec((None, D_MODEL, tf), lambda i, f: (i_dense, 0, f)),
                  pl.BlockSpec((None, tf, D_MODEL), lambda i, f: (i_dense, f, 0)),
                  pl.BlockSpec((None, 1, D_MODEL), lambda i, f: (l, 0, 0)),
                  pl.BlockSpec((None, 1, D_MODEL), lambda i, f: (l, 0, 0))],
        out_specs=pl.BlockSpec((tm, D_MODEL), lambda i, f: (i, 0)),
        out_shape=jax.ShapeDtypeStruct((n, D_MODEL), F32),
        scratch_shapes=[pltpu.VMEM((tm, D_MODEL), F32)],
        compiler_params=_cparams(2), name="ffn_dense",
    )(x, w["wg_d"], w["wu_d"], w["wd_d"], w["ln2g"], w["ln2b"])


def _router_kernel(x_ref, wr_ref, br_ref, gates_ref):
    logits = jnp.dot(x_ref[...], wr_ref[...], preferred_element_type=F32,
                     precision=lax.Precision.HIGHEST) + br_ref[...]
    lane = lax.broadcasted_iota(I32, logits.shape, 1).astype(F32)
    logits = jnp.where(lane < N_EXPERTS, logits, -jnp.inf)
    m1 = jnp.max(logits, axis=1, keepdims=True)
    i1 = jnp.min(jnp.where(logits == m1, lane, float(LANES)), axis=1, keepdims=True)
    rest = jnp.where(lane == i1, -jnp.inf, logits)
    m2 = jnp.max(rest, axis=1, keepdims=True)
    i2 = jnp.min(jnp.where(rest == m2, lane, float(LANES)), axis=1, keepdims=True)
    e2 = jnp.exp(m2 - m1)
    g1 = 1.0 / (1.0 + e2)
    g2 = e2 / (1.0 + e2)
    for e in range(N_EXPERTS):
        gate = jnp.where(i1 == e, g1, 0.0) + jnp.where(i2 == e, g2, 0.0)
        gates_ref[e] = jnp.broadcast_to(gate, (gate.shape[0], LANES))


def _router(x, w, i_moe):
    n = x.shape[0]
    tm = TOKEN_BLOCK
    return pl.pallas_call(
        _router_kernel,
        grid=(n // tm,),
        in_specs=[pl.BlockSpec((tm, D_MODEL), lambda i: (i, 0)),
                  pl.BlockSpec((None, D_MODEL, LANES), lambda i: (i_moe, 0, 0)),
                  pl.BlockSpec((None, 1, LANES), lambda i: (i_moe, 0, 0))],
        out_specs=pl.BlockSpec((N_EXPERTS, tm, LANES), lambda i: (0, i, 0)),
        out_shape=jax.ShapeDtypeStruct((N_EXPERTS, n, LANES), F32),
        compiler_params=_cparams(1), name="moe_router",
    )(x, w["wr"], w["br"])


def _moe_kernel(x_ref, gate_ref, wg_ref, wu_ref, wd_ref, g_ref, b_ref, out_ref, acc_ref):
    e = pl.program_id(1)
    f = pl.program_id(2)

    @pl.when((e == 0) & (f == 0))
    def _():
        acc_ref[...] = jnp.zeros(acc_ref.shape, F32)

    x = x_ref[...].astype(BF16)
    hid = jax.nn.silu(_dot(x, wg_ref[...])) * _dot(x, wu_ref[...])
    y = _dot(hid.astype(BF16), wd_ref[...])
    gate = jnp.concatenate([gate_ref[...]] * (D_MODEL // LANES), axis=1)
    acc_ref[...] += gate * y

    @pl.when((e == pl.num_programs(1) - 1) & (f == pl.num_programs(2) - 1))
    def _():
        out_ref[...] = _layer_norm(ALPHA * x_ref[...] + acc_ref[...], g_ref[...], b_ref[...])


def _ffn_moe(x, gates, w, i_moe, l):
    n = x.shape[0]
    tm = TOKEN_BLOCK
    d_ff = w["wg_e"].shape[-1]
    tf = _ff_chunk(d_ff)
    return pl.pallas_call(
        _moe_kernel,
        grid=(n // tm, N_EXPERTS, d_ff // tf),
        in_specs=[pl.BlockSpec((tm, D_MODEL), lambda i, e, f: (i, 0)),
                  pl.BlockSpec((None, tm, LANES), lambda i, e, f: (e, i, 0)),
                  pl.BlockSpec((None, None, D_MODEL, tf), lambda i, e, f: (i_moe, e, 0, f)),
                  pl.BlockSpec((None, None, D_MODEL, tf), lambda i, e, f: (i_moe, e, 0, f)),
                  pl.BlockSpec((None, None, tf, D_MODEL), lambda i, e, f: (i_moe, e, f, 0)),
                  pl.BlockSpec((None, 1, D_MODEL), lambda i, e, f: (l, 0, 0)),
                  pl.BlockSpec((None, 1, D_MODEL), lambda i, e, f: (l, 0, 0))],
        out_specs=pl.BlockSpec((tm, D_MODEL), lambda i, e, f: (i, 0)),
        out_shape=jax.ShapeDtypeStruct((n, D_MODEL), F32),
        scratch_shapes=[pltpu.VMEM((tm, D_MODEL), F32)],
        compiler_params=_cparams(3), name="ffn_moe",
    )(x, gates, w["wg_e"], w["wu_e"], w["wd_e"], w["ln2g"], w["ln2b"])


def _swap_halves(w):
    half = w.shape[-1] // 2
    return jnp.concatenate([w[..., half:], w[..., :half]], axis=-1)


def _prepare_weights(w_in, idx_ln_g, idx_ln_b, q_norm_g, w_uq, kv_norm_g, w_uk, w_uv, w_out, ln1_g, ln1_b,
                     ln2_g, ln2_b, w_gate_dense, w_up_dense, w_down_dense, w_router, b_router,
                     w_gate_exp, w_up_exp, w_down_exp):
    bf = lambda a: a.astype(BF16)
    depth = w_in.shape[0]
    splits = (H_A * DH_A, DH_A, DH_A, H_IDX * D_IDX, D_IDX, H_IDX, Q_LORA, KV_LORA, D_ROPE)
    cols, start = [], 0
    for width in splits:
        cols.append(w_in[:, :, start:start + width])
        start += width
    w_qa, w_ka, w_va, w_qidx, w_kidx, w_widx, w_cq, w_ckv, w_kr = cols
    w_uq4 = w_uq.reshape(depth, Q_LORA, H_B, D_NOPE + D_ROPE)
    w_uq_rope = w_uq4[..., D_NOPE:]
    vec = lambda a: a[:, None, :]
    return {
        "wqa": bf(w_qa),
        "wkva": bf(jnp.concatenate([w_ka, w_va], axis=-1)),
        "wqidx": bf(w_qidx.reshape(depth, D_MODEL, H_IDX, D_IDX).transpose(0, 2, 1, 3)),
        "wkidx": bf(w_kidx),
        "wwidx": bf(jnp.pad(w_widx, ((0, 0), (0, 0), (0, LANES - H_IDX)))),
        "wcq": bf(w_cq),
        "wckv": bf(w_ckv),
        "wkr": bf(jnp.stack([w_kr, _swap_halves(w_kr)], axis=1)),
        "idxg": vec(idx_ln_g), "idxb": vec(idx_ln_b), "qng": vec(q_norm_g), "kvng": vec(kv_norm_g),
        "wuqn": bf(w_uq4[..., :D_NOPE].transpose(0, 2, 1, 3)),
        "wuqr": bf(w_uq_rope.transpose(0, 2, 1, 3)),
        "wuqrs": bf(_swap_halves(w_uq_rope).transpose(0, 2, 1, 3)),
        "wukt": bf(w_uk.transpose(0, 2, 3, 1)),
        "wuv": bf(w_uv.transpose(0, 2, 1, 3)),
        "wout": bf(w_out),
        "ln1g": vec(ln1_g), "ln1b": vec(ln1_b), "ln2g": vec(ln2_g), "ln2b": vec(ln2_b),
        "wg_d": bf(w_gate_dense), "wu_d": bf(w_up_dense), "wd_d": bf(w_down_dense),
        "wr": jnp.pad(w_router, ((0, 0), (0, 0), (0, LANES - N_EXPERTS))),
        "br": jnp.pad(b_router, ((0, 0), (0, LANES - N_EXPERTS)))[:, None, :],
        "wg_e": bf(w_gate_exp), "wu_e": bf(w_up_exp), "wd_e": bf(w_down_exp),
    }


def _rope_tables(pos):
    half = D_ROPE // 2
    inv = ROPE_THETA ** (-jnp.arange(half, dtype=F32) / half)
    ang = pos.astype(F32)[:, None] * inv[None, :]
    cos, sin = jnp.cos(ang), jnp.sin(ang)
    return jnp.concatenate([cos, cos], axis=-1), jnp.concatenate([-sin, sin], axis=-1)


def _pad_new(a, db, t_new):
    a = a.reshape(db, t_new, a.shape[-1])
    return jnp.pad(a, ((0, 0), (0, LANES - t_new), (0, 0)))


def kernel(x_prompt, x_sample, cache_a_kv, cache_a_kidx, cache_b_latent, page_table, rel_bias, w_in, idx_ln_g, idx_ln_b, q_norm_g, w_uq, kv_norm_g, w_uk, w_uv, w_out, ln1_g, ln1_b, ln2_g, ln2_b, w_gate_dense, w_up_dense, w_down_dense, w_router, b_router, w_gate_exp, w_up_exp, w_down_exp):
    batch, seq, _ = x_prompt.shape
    db, t_new, _ = x_sample.shape
    depth = w_in.shape[0]
    n_prompt = batch * seq
    n_sample = db * t_new
    n = n_prompt + n_sample
    past_len = page_table.shape[1] * PAGE_SIZE
    assert seq % KEY_CHUNK == 0 and seq % MLA_KEY_CHUNK == 0 and n % TOKEN_BLOCK == 0 and n_prompt % t_new == 0 and t_new % 8 == 0
    assert page_table.shape[1] % min(PAGES_PER_STEP, page_table.shape[1]) == 0

    w = _prepare_weights(w_in, idx_ln_g, idx_ln_b, q_norm_g, w_uq, kv_norm_g, w_uk, w_uv, w_out, ln1_g, ln1_b,
                         ln2_g, ln2_b, w_gate_dense, w_up_dense, w_down_dense, w_router, b_router,
                         w_gate_exp, w_up_exp, w_down_exp)
    pos = jnp.concatenate([jnp.tile(jnp.arange(seq, dtype=I32), batch),
                           jnp.tile(past_len + jnp.arange(t_new, dtype=I32), db)])
    cos_t, sin_t = _rope_tables(pos)
    bias_tile, bias_past, bias_new = _bias_tables(rel_bias, past_len, t_new)

    cache_kidx_t = jnp.swapaxes(cache_a_kidx, 2, 3)
    cache_lat_t = jnp.swapaxes(cache_b_latent, 2, 3)

    x = jnp.concatenate([x_prompt.reshape(n_prompt, D_MODEL), x_sample.reshape(n_sample, D_MODEL)], axis=0)
    kva_all, kidx_all, kvb_all = [], [], []
    for l in range(depth):
        qa, kva, kva_bf, qidx, kidx, kidx_bf, widx, qcat, kvb, kvb_bf = _projections(x, cos_t, sin_t, w, l)
        a_p = _dsa_prompt(qa, qidx, widx, kidx_bf, kva_bf, bias_tile, batch, seq)
        o_p = _mla_prompt(qcat, kvb_bf, batch, seq)
        qidx_s = qidx[:, n_prompt:].astype(F32)
        qa_s = qa[n_prompt:].astype(F32)
        qcat_s = qcat[:, n_prompt:].astype(F32)
        sel = _samp_idx(page_table, qidx_s, widx, _pad_new(kidx_bf[n_prompt:], db, t_new), cache_kidx_t, l,
                        n_prompt, t_new)
        a_s = _samp_dsa(page_table, qa_s, sel, _pad_new(kva_bf[n_prompt:], db, t_new), bias_past, bias_new,
                        cache_a_kv, l, t_new)
        o_s = _samp_mla(page_table, qcat_s, _pad_new(kvb_bf[n_prompt:], db, t_new), cache_lat_t, l, t_new)
        a = jnp.concatenate([a_p, a_s], axis=0)
        o_lat = jnp.concatenate([o_p, o_s], axis=1)
        x = _merge(x, a, o_lat, w, l)
        if l % 2 == 0:
            x = _ffn_dense(x, w, l // 2, l)
        else:
            x = _ffn_moe(x, _router(x, w, l // 2), w, l // 2, l)
        kva_all.append(kva)
        kidx_all.append(kidx)
        kvb_all.append(kvb)

    def split(parts, width):
        full = jnp.stack(parts)
        return (full[:, :n_prompt].reshape(depth, batch, seq, width),
                full[:, n_prompt:].reshape(depth, db, t_new, width))

    kva_p, kva_s = split(kva_all, C_A)
    kidx_p, kidx_s = split(kidx_all, D_IDX)
    kvb_p, kvb_s = split(kvb_all, C_B)
    return (x[:n_prompt].reshape(batch, seq, D_MODEL), x[n_prompt:].reshape(db, t_new, D_MODEL),
            kva_p, kidx_p, kvb_p, kva_s, kidx_s, kvb_s)
```

```python
import functools
import math

import jax
import jax.numpy as jnp
from jax import lax
from jax.experimental import pallas as pl
from jax.experimental.pallas import tpu as pltpu

F32 = jnp.float32
BF16 = jnp.bfloat16
I32 = jnp.int32

D_MODEL = 1024
DEPTH = 4
PAGE_SIZE = 128
H_A = 4
DH_A = 128
H_IDX = 8
D_IDX = 64
TOPK_MAX = 256
N_BUCKETS = 32
T5_MAX_DIST = 128
H_B = 4
D_NOPE = 128
D_ROPE = 64
D_V = 128
Q_LORA = 384
KV_LORA = 256
ROPE_THETA = 10000.0
C_A = 2 * DH_A
C_B = KV_LORA + D_ROPE
N_EXPERTS = 8
ALPHA = (2 * DEPTH) ** 0.25
EPS = 1e-5
Q_BLOCK = 128
MLA_SCALE = (D_NOPE + D_ROPE) ** -0.5
LOG2E = math.log2(math.e)
QA_SCALE = DH_A ** -0.5 * LOG2E
QCAT_SCALE = MLA_SCALE * LOG2E

LANES = 128
TOKEN_BLOCK = 512
KEY_CHUNK = 512
SCORE_TILE = 256
BIAS_TILE = 256
MLA_KEY_CHUNK = 512
PAGES_PER_STEP = 64
VMEM_LIMIT = 56 * 1024 * 1024

assert 3 * Q_BLOCK - (BIAS_TILE - 1) >= T5_MAX_DIST and KEY_CHUNK % BIAS_TILE == 0 and KEY_CHUNK % SCORE_TILE == 0

INT_MIN = -(2 ** 31)
NEG_BIG = -1e30
NT_DIMS = (((1,), (1,)), ((), ()))


def _cparams(n_axes):
    return pltpu.CompilerParams(dimension_semantics=("arbitrary",) * n_axes,
                                vmem_limit_bytes=VMEM_LIMIT)


def _dot(a, b):
    return jnp.dot(a, b, preferred_element_type=F32)


def _dot_nt(a, b):
    return lax.dot_general(a, b, NT_DIMS, preferred_element_type=F32)


def _layer_norm(x, g, b):
    xc = x - jnp.mean(x, axis=-1, keepdims=True)
    y = xc * lax.rsqrt(jnp.mean(xc * xc, axis=-1, keepdims=True) + EPS)
    return y * g + b


def _rms_norm(x, g):
    return x * lax.rsqrt(jnp.mean(x * x, axis=-1, keepdims=True) + EPS) * g


def _order_key(score):
    bits = lax.bitcast_convert_type(score, I32)
    return jnp.where(bits < 0, bits ^ jnp.int32(0x7FFFFFFF), bits)


def _bias_of_dist(dist, relb_ref, h):
    n = jnp.maximum(dist, 0)
    max_exact = N_BUCKETS // 2
    nf = jnp.maximum(n, max_exact).astype(F32)
    large = max_exact + (jnp.log(nf / max_exact) / math.log(T5_MAX_DIST / max_exact)
                         * (N_BUCKETS - max_exact)).astype(I32)
    bucket = jnp.where(n < max_exact, n, jnp.minimum(large, N_BUCKETS - 1))
    out = jnp.zeros(dist.shape, F32)
    for b in range(N_BUCKETS):
        out = jnp.where(bucket == b, relb_ref[b, h], out)
    return out * LOG2E


def _bias_tables_kernel(relb_ref, tile_ref, past_ref, new_ref, *, past_len, t_new):
    r = lax.broadcasted_iota(I32, (Q_BLOCK, BIAS_TILE), 0)
    c = lax.broadcasted_iota(I32, (Q_BLOCK, BIAS_TILE), 1)
    for m in range(4):
        for h in range(H_A):
            tile_ref[m, h] = _bias_of_dist(Q_BLOCK * m + r - c, relb_ref, h) - relb_ref[N_BUCKETS - 1, h] * LOG2E
    q = lax.broadcasted_iota(I32, (t_new, past_len), 0)
    s = lax.broadcasted_iota(I32, (t_new, past_len), 1)
    qn = lax.broadcasted_iota(I32, (t_new, LANES), 0)
    sn = lax.broadcasted_iota(I32, (t_new, LANES), 1)
    for h in range(H_A):
        past_ref[h * t_new:(h + 1) * t_new, :] = _bias_of_dist(past_len + q - s, relb_ref, h)
        new_ref[h * t_new:(h + 1) * t_new, :] = _bias_of_dist(qn - sn, relb_ref, h)


def _bias_tables(rel_bias, past_len, t_new):
    return pl.pallas_call(
        functools.partial(_bias_tables_kernel, past_len=past_len, t_new=t_new),
        out_shape=(jax.ShapeDtypeStruct((4, H_A, Q_BLOCK, BIAS_TILE), F32),
                   jax.ShapeDtypeStruct((H_A * t_new, past_len), F32),
                   jax.ShapeDtypeStruct((H_A * t_new, LANES), F32)),
        in_specs=[pl.BlockSpec(memory_space=pltpu.SMEM)],
        name="bias_tables",
    )(rel_bias)


def _proj_kernel(x_ref, cos_ref, sin_ref, wqa, wkva, wqidx, wkidx, wwidx, wcq, wckv, wkr,
                 idxg, idxb, qng, kvng, wuqn, wuqr, wuqrs, wukt,
                 qa_o, kva_o, kvabf_o, qidx_o, kidx_o, kidxbf_o, widx_o, qcat_o, kvb_o, kvbbf_o):
    x = x_ref[...].astype(BF16)
    cos = cos_ref[...]
    sin = sin_ref[...]
    qa_o[...] = (_dot(x, wqa[...]) * QA_SCALE).astype(BF16)
    kva = _dot(x, wkva[...])
    kva_o[...] = kva
    kvabf_o[...] = kva.astype(BF16)
    for h in range(H_IDX):
        qidx_o[h] = (_dot(x, wqidx[h]) * (D_IDX ** -0.5)).astype(BF16)
    kidx = _layer_norm(_dot(x, wkidx[...]), idxg[...], idxb[...])
    kidx_o[...] = kidx
    kidxbf_o[...] = kidx.astype(BF16)
    widx_o[...] = _dot(x, wwidx[...])[:, :H_IDX] * (H_IDX ** -0.5)
    cq = _rms_norm(_dot(x, wcq[...]), qng[...]).astype(BF16)
    for h in range(H_B):
        q_nope = _dot(cq, wuqn[h]).astype(BF16)
        q_lat = _dot(q_nope, wukt[h])
        q_rope = _dot(cq, wuqr[h]) * cos + _dot(cq, wuqrs[h]) * sin
        qcat_o[h, :, 0:KV_LORA] = (q_lat * QCAT_SCALE).astype(BF16)
        qcat_o[h, :, KV_LORA:C_B] = (q_rope * QCAT_SCALE).astype(BF16)
    ckv = _rms_norm(_dot(x, wckv[...]), kvng[...])
    k_rope = _dot(x, wkr[0]) * cos + _dot(x, wkr[1]) * sin
    kvb_o[:, 0:KV_LORA] = ckv
    kvb_o[:, KV_LORA:C_B] = k_rope
    kvbbf_o[:, 0:KV_LORA] = ckv.astype(BF16)
    kvbbf_o[:, KV_LORA:C_B] = k_rope.astype(BF16)


def _projections(x, cos_t, sin_t, w, l):
    n = x.shape[0]
    tm = TOKEN_BLOCK
    row = lambda width: pl.BlockSpec((tm, width), lambda i: (i, 0))
    lay2 = lambda a: pl.BlockSpec((None,) + a.shape[1:], lambda i: (l,) + (0,) * (a.ndim - 1))
    weights = [w["wqa"], w["wkva"], w["wqidx"], w["wkidx"], w["wwidx"], w["wcq"], w["wckv"], w["wkr"],
               w["idxg"], w["idxb"], w["qng"], w["kvng"], w["wuqn"], w["wuqr"], w["wuqrs"], w["wukt"]]
    out_shape = (
        jax.ShapeDtypeStruct((n, H_A * DH_A), BF16),
        jax.ShapeDtypeStruct((n, C_A), F32),
        jax.ShapeDtypeStruct((n, C_A), BF16),
        jax.ShapeDtypeStruct((H_IDX, n, D_IDX), BF16),
        jax.ShapeDtypeStruct((n, D_IDX), F32),
        jax.ShapeDtypeStruct((n, D_IDX), BF16),
        jax.ShapeDtypeStruct((n, H_IDX), F32),
        jax.ShapeDtypeStruct((H_B, n, C_B), BF16),
        jax.ShapeDtypeStruct((n, C_B), F32),
        jax.ShapeDtypeStruct((n, C_B), BF16),
    )
    out_specs = (row(H_A * DH_A), row(C_A), row(C_A),
                 pl.BlockSpec((H_IDX, tm, D_IDX), lambda i: (0, i, 0)),
                 row(D_IDX), row(D_IDX), row(H_IDX),
                 pl.BlockSpec((H_B, tm, C_B), lambda i: (0, i, 0)),
                 row(C_B), row(C_B))
    return pl.pallas_call(
        _proj_kernel,
        grid=(n // tm,),
        in_specs=[row(D_MODEL), row(D_ROPE), row(D_ROPE)] + [lay2(a) for a in weights],
        out_specs=out_specs,
        out_shape=out_shape,
        compiler_params=_cparams(1),
        name="projections",
    )(x, cos_t, sin_t, *weights)


def _lane_tile(x, width):
    reps = width // LANES
    return x if reps == 1 else jnp.concatenate([x] * reps, axis=1)


def _softmax_init(m_ref, l_ref, acc_ref):
    m_ref[...] = jnp.full(m_ref.shape, NEG_BIG, F32)
    l_ref[...] = jnp.zeros(l_ref.shape, F32)
    acc_ref[...] = jnp.zeros(acc_ref.shape, F32)


def _softmax_step(s, v, m_ref, l_ref, acc_ref, v_transposed=False):
    m_old = m_ref[...]
    m_new = jnp.maximum(m_old, jnp.max(s, axis=1, keepdims=True))
    p = jnp.exp2(s - _lane_tile(m_new, s.shape[1]))
    alpha = jnp.exp2(m_old - m_new)
    l_ref[...] = alpha * l_ref[...] + jnp.sum(p, axis=1, keepdims=True)
    pv = _dot_nt(p.astype(BF16), v) if v_transposed else _dot(p.astype(BF16), v)
    acc_ref[...] = _lane_tile(alpha, acc_ref.shape[1]) * acc_ref[...] + pv
    m_ref[...] = m_new


def _select_threshold(keys_ref, n_rows, n_chunks, chunk, n_keep, col_bits, bracket=None):
    whole = n_chunks is None
    lane_iota = lax.broadcasted_iota(I32, (n_rows, LANES), 1)

    def count(pred):
        if whole:
            ind = [jnp.where(pred(keys_ref[:, g * LANES:(g + 1) * LANES], g * LANES), 1.0, 0.0)
                   for g in range(chunk // LANES)]
            while len(ind) > 1:
                ind = [a + b for a, b in zip(ind[0::2], ind[1::2])] + ([ind[-1]] if len(ind) % 2 else [])
            return jnp.sum(ind[0], axis=1, keepdims=True)

        def body(j, c):
            off = pl.multiple_of(j * chunk, chunk)
            for g in range(chunk // LANES):
                col0 = off + g * LANES
                c = c + jnp.where(pred(keys_ref[:, pl.ds(pl.multiple_of(col0, LANES), LANES)], col0), 1.0, 0.0)
            return c
        c = lax.fori_loop(0, n_chunks, body, jnp.zeros((n_rows, LANES), F32))
        return jnp.sum(c, axis=1, keepdims=True)

    def lanes(col):
        return jnp.broadcast_to(col, (n_rows, LANES))

    if bracket is None:
        def bit_body(p, prefix):
            cand = prefix + lax.shift_left(jnp.int32(1), 31 - p)
            cand_b = lanes(cand)
            cnt = count(lambda t, col0: t >= cand_b)
            return jnp.where(cnt >= n_keep, cand, prefix)

        thr = lax.fori_loop(0, 32, bit_body, jnp.full((n_rows, 1), INT_MIN, I32))
    else:
        def probe(state):
            lo, hi, thr, done, _ = state
            mid = (lo >> 1) + (hi >> 1) + (lo & hi & 1)
            mid_b = lanes(mid)
            cnt = count(lambda t, col0: t >= mid_b)
            enough = cnt >= n_keep
            exact = cnt == n_keep
            thr = jnp.where(exact & (done == 0), mid, thr)
            done = jnp.where(exact, 1, done)
            lo = jnp.where(enough, mid, lo)
            hi = jnp.where(enough, hi, mid)
            still = jnp.max(jnp.where((done == 0) & (hi > lo + 1), 1.0, 0.0))
            return lo, hi, thr, done, still

        lo0, hi0 = bracket
        state = (lo0, hi0, lo0, jnp.zeros((n_rows, 1), I32), jnp.max(jnp.where(hi0 > lo0 + 1, 1.0, 0.0)))
        lo, _, thr, done, _ = lax.while_loop(lambda state: state[4] > 0.0, probe, state)
        thr = jnp.where(done == 1, thr, lo)
    thr = jnp.maximum(thr, INT_MIN + 1)
    thr_b = lanes(thr)
    n_gt = count(lambda t, col0: t > thr_b)
    n_eq = count(lambda t, col0: t == thr_b)
    need = n_keep - n_gt
    tie = jnp.max(jnp.where(n_eq > need, 1.0, 0.0)) > 0.0

    @pl.when(tie)
    def _():
        def cut_body(p, cut):
            cand = cut + lax.shift_left(jnp.int32(1), col_bits - 1 - p)
            cand_b = lanes(cand)
            cnt = count(lambda t, col0: (t == thr_b) & (col0 + lane_iota < cand_b))
            return jnp.where(cnt < need, cand, cut)

        cut_b = lanes(lax.fori_loop(0, col_bits, cut_body, jnp.zeros((n_rows, 1), I32)))

        def demote_tile(col0):
            t = keys_ref[:, pl.ds(col0, LANES)]
            lose = (t == thr_b) & (col0 + lane_iota > cut_b)
            keys_ref[:, pl.ds(col0, LANES)] = jnp.where(lose, thr_b - 1, t)

        if whole:
            for g in range(chunk // LANES):
                demote_tile(g * LANES)
        else:
            def demote(j, carry):
                for g in range(chunk // LANES):
                    demote_tile(pl.multiple_of(j * chunk + g * LANES, LANES))
                return carry
            lax.fori_loop(0, n_chunks, demote, 0)

    return thr


def _dsa_prompt_kernel(qidx_ref, widx_ref, qa_ref, kidx_ref, kva_ref, bias_ref, out_ref,
                       keys_ref, wb_ref, gmax_ref, qs_ref, m_ref, l_ref, acc_ref, *, n_keep, col_bits):
    i = pl.program_id(1)
    kc = KEY_CHUNK
    st = SCORE_TILE
    n_chunks = i // (kc // Q_BLOCK) + 1
    row_pos = i * Q_BLOCK + lax.broadcasted_iota(I32, (Q_BLOCK, st), 0)
    lane = lax.broadcasted_iota(I32, (Q_BLOCK, st), 1)

    for h in range(H_IDX):
        wb_ref[h] = jnp.broadcast_to(widx_ref[:, h:h + 1], (Q_BLOCK, st))
    gmax_ref[...] = jnp.full(gmax_ref.shape, -jnp.inf, F32)

    def score_body(j, carry):
        for sub in range(kc // st):
            off = pl.multiple_of(j * kc + sub * st, st)
            keys_c = kidx_ref[pl.ds(off, st), :]
            score = jnp.zeros((Q_BLOCK, st), F32)
            for h in range(H_IDX):
                score = score + wb_ref[h] * jnp.maximum(_dot_nt(qidx_ref[h], keys_c), 0.0)
            visible = off + lane <= row_pos
            keys_ref[:, pl.ds(off, st)] = jnp.where(visible, _order_key(score), INT_MIN)
            gmax_ref[...] = jnp.maximum(gmax_ref[...], jnp.where(visible, score, -jnp.inf))
        return carry

    lax.fori_loop(0, n_chunks, score_body, 0)
    gmax = gmax_ref[...]
    lo0 = _order_key(jnp.min(gmax, axis=1, keepdims=True))
    hi0 = _order_key(jnp.max(gmax, axis=1, keepdims=True)) + 1
    thr = _select_threshold(keys_ref, Q_BLOCK, n_chunks, kc, n_keep, col_bits, bracket=(lo0, hi0))

    rows = H_A * Q_BLOCK
    _softmax_init(m_ref, l_ref, acc_ref)
    for h in range(H_A):
        qs_ref[h * Q_BLOCK:(h + 1) * Q_BLOCK, :] = qa_ref[:, h * DH_A:(h + 1) * DH_A]
    thr_b = jnp.broadcast_to(thr, (Q_BLOCK, kc))
    tiles_per_chunk = kc // BIAS_TILE
    n_far = jnp.maximum(n_chunks - 2, 0)

    def logits(j):
        return _dot_nt(qs_ref[...], kva_ref[pl.ds(pl.multiple_of(j * kc, kc), kc), 0:DH_A])

    def attend(j, s_raw, near):
        s_next = logits(jnp.minimum(j + 1, n_chunks - 1))
        off = pl.multiple_of(j * kc, kc)
        add = jnp.where(keys_ref[:, pl.ds(off, kc)] >= thr_b, 0.0, NEG_BIG)[None]
        if near:
            m = i - (kc // Q_BLOCK) * j
            tiles = [bias_ref[jnp.clip(m - (BIAS_TILE // Q_BLOCK) * t, 0, 3)] for t in range(tiles_per_chunk)]
            add = add + jnp.concatenate(tiles, axis=2)
        s = (s_raw.reshape(H_A, Q_BLOCK, kc) + add).reshape(rows, kc)
        _softmax_step(s, kva_ref[pl.ds(off, kc), DH_A:C_A], m_ref, l_ref, acc_ref)
        return s_next

    s_raw = lax.fori_loop(0, n_far, lambda j, s: attend(j, s, False), logits(0))
    lax.fori_loop(n_far, n_chunks, lambda j, s: attend(j, s, True), s_raw)
    o = acc_ref[...] / l_ref[...]
    for h in range(H_A):
        out_ref[:, h * DH_A:(h + 1) * DH_A] = o[h * Q_BLOCK:(h + 1) * Q_BLOCK, :]


def _dsa_prompt(qa, qidx, widx, kidx_bf, kva_bf, bias_tile, batch, seq):
    nq = seq // Q_BLOCK
    n_keep = min(TOPK_MAX, seq // 4)
    assert n_keep <= SCORE_TILE and DH_A == LANES
    kern = functools.partial(_dsa_prompt_kernel, n_keep=n_keep, col_bits=max(1, (seq - 1).bit_length()))
    return pl.pallas_call(
        kern,
        grid=(batch, nq),
        in_specs=[
            pl.BlockSpec((H_IDX, Q_BLOCK, D_IDX), lambda b, i: (0, b * nq + i, 0)),
            pl.BlockSpec((Q_BLOCK, H_IDX), lambda b, i: (b * nq + i, 0)),
            pl.BlockSpec((Q_BLOCK, H_A * DH_A), lambda b, i: (b * nq + i, 0)),
            pl.BlockSpec((seq, D_IDX), lambda b, i: (b, 0)),
            pl.BlockSpec((seq, C_A), lambda b, i: (b, 0)),
            pl.BlockSpec(bias_tile.shape, lambda b, i: (0, 0, 0, 0)),
        ],
        out_specs=pl.BlockSpec((Q_BLOCK, H_A * DH_A), lambda b, i: (b * nq + i, 0)),
        out_shape=jax.ShapeDtypeStruct((batch * seq, H_A * DH_A), F32),
        scratch_shapes=[
            pltpu.VMEM((Q_BLOCK, seq), I32),
            pltpu.VMEM((H_IDX, Q_BLOCK, SCORE_TILE), F32),
            pltpu.VMEM((Q_BLOCK, SCORE_TILE), F32),
            pltpu.VMEM((H_A * Q_BLOCK, DH_A), BF16),
            pltpu.VMEM((H_A * Q_BLOCK, LANES), F32),
            pltpu.VMEM((H_A * Q_BLOCK, LANES), F32),
            pltpu.VMEM((H_A * Q_BLOCK, DH_A), F32),
        ],
        compiler_params=_cparams(2),
        name="dsa_prompt",
    )(qidx, widx, qa, kidx_bf, kva_bf, bias_tile)


def _mla_prompt_kernel(q_ref, kvb_ref, out_ref, m_ref, l_ref, acc_ref):
    i = pl.program_id(1)
    kc = MLA_KEY_CHUNK
    rows = H_B * Q_BLOCK
    n_full = i // (kc // Q_BLOCK)
    q = q_ref[...].reshape(rows, C_B)
    _softmax_init(m_ref, l_ref, acc_ref)

    def keys(j):
        return kvb_ref[pl.ds(pl.multiple_of(j * kc, kc), kc), :]

    def body(j, s):
        s_next = _dot_nt(q, keys(j + 1))
        _softmax_step(s, keys(j)[:, :KV_LORA], m_ref, l_ref, acc_ref)
        return s_next

    s_last = lax.fori_loop(0, n_full, body, _dot_nt(q, keys(0)))
    row_pos = i * Q_BLOCK + lax.broadcasted_iota(I32, (H_B, Q_BLOCK, kc), 1).reshape(rows, kc)
    visible = n_full * kc + lax.broadcasted_iota(I32, (rows, kc), 1) <= row_pos
    s_last = jnp.where(visible, s_last, NEG_BIG)
    _softmax_step(s_last, keys(n_full)[:, :KV_LORA], m_ref, l_ref, acc_ref)
    out_ref[...] = (acc_ref[...] / _lane_tile(l_ref[...], KV_LORA)).reshape(H_B, Q_BLOCK, KV_LORA)


def _mla_prompt(qcat, kvb_bf, batch, seq):
    nq = seq // Q_BLOCK
    rows = H_B * Q_BLOCK
    return pl.pallas_call(
        _mla_prompt_kernel,
        grid=(batch, nq),
        in_specs=[
            pl.BlockSpec((H_B, Q_BLOCK, C_B), lambda b, i: (0, b * nq + i, 0)),
            pl.BlockSpec((seq, C_B), lambda b, i: (b, 0)),
        ],
        out_specs=pl.BlockSpec((H_B, Q_BLOCK, KV_LORA), lambda b, i: (0, b * nq + i, 0)),
        out_shape=jax.ShapeDtypeStruct((H_B, batch * seq, KV_LORA), F32),
        scratch_shapes=[
            pltpu.VMEM((rows, LANES), F32),
            pltpu.VMEM((rows, LANES), F32),
            pltpu.VMEM((rows, KV_LORA), F32),
        ],
        compiler_params=_cparams(2),
        name="mla_prompt",
    )(qcat, kvb_bf)


def _page_specs(l, page_shape, pps):
    def spec(p):
        return pl.BlockSpec((None, None) + page_shape,
                            lambda b, c, pt: (l, pt[b, c * pps + p], 0, 0))
    return [spec(p) for p in range(pps)]


def _samp_idx_kernel(pt_ref, qidx_ref, widx_ref, knew_ref, *rest, pps, past_len, t_new, n_keep, col_bits):
    pages = rest[:pps]
    sel_ref, kbuf, keys_ref, gmax_ref = rest[pps:]
    c = pl.program_id(1)
    n_steps = pl.num_programs(1)
    total = past_len + LANES
    span = pps * PAGE_SIZE
    q = qidx_ref[...].reshape(H_IDX * t_new, D_IDX).astype(BF16)

    def weighted(r):
        score = jnp.zeros((t_new, r.shape[1]), F32)
        for h in range(H_IDX):
            score = score + jnp.broadcast_to(widx_ref[:, h:h + 1], (t_new, r.shape[1])) * r[h * t_new:(h + 1) * t_new, :]
        return score

    @pl.when(c == 0)
    def _():
        gmax_ref[...] = jnp.full(gmax_ref.shape, -jnp.inf, F32)

    for p in range(pps):
        kbuf[:, p * PAGE_SIZE:(p + 1) * PAGE_SIZE] = pages[p][...].astype(BF16)
    off = pl.multiple_of(c * span, span)
    score = weighted(jnp.maximum(_dot(q, kbuf[...]), 0.0))
    keys_ref[:, pl.ds(off, span)] = _order_key(score)
    gmax = gmax_ref[...]
    for t in range(span // SCORE_TILE):
        gmax = jnp.maximum(gmax, score[:, t * SCORE_TILE:(t + 1) * SCORE_TILE])
    gmax_ref[...] = gmax

    @pl.when(c == n_steps - 1)
    def _():
        qi = lax.broadcasted_iota(I32, (t_new, LANES), 0)
        ki = lax.broadcasted_iota(I32, (t_new, LANES), 1)
        score_new = weighted(jnp.maximum(_dot_nt(q, knew_ref[...]), 0.0))
        keys_ref[:, past_len:total] = jnp.where(ki <= qi, _order_key(score_new), INT_MIN)
        lo0 = _order_key(jnp.min(gmax, axis=1, keepdims=True))
        top = jnp.maximum(jnp.max(gmax, axis=1, keepdims=True),
                          jnp.max(jnp.where(ki <= qi, score_new, -jnp.inf), axis=1, keepdims=True))
        thr = _select_threshold(keys_ref, t_new, None, total, n_keep, col_bits,
                                bracket=(lo0, _order_key(top) + 1))
        sel_ref[...] = jnp.where(keys_ref[...] >= thr, 1.0, 0.0)


def _samp_idx(page_table, qidx, widx, kidx_new_pad, cache, l, n_prompt, t_new):
    db, n_pages = page_table.shape
    pps = min(PAGES_PER_STEP, n_pages)
    past_len = n_pages * PAGE_SIZE
    total = past_len + LANES
    n_keep = min(TOPK_MAX, (past_len + t_new) // 4)
    assert (pps * PAGE_SIZE) % SCORE_TILE == 0 and n_keep <= SCORE_TILE
    base = n_prompt // t_new
    kern = functools.partial(_samp_idx_kernel, pps=pps, past_len=past_len, t_new=t_new, n_keep=n_keep,
                             col_bits=max(1, (total - 1).bit_length()))
    grid_spec = pltpu.PrefetchScalarGridSpec(
        num_scalar_prefetch=1,
        grid=(db, n_pages // pps),
        in_specs=[
            pl.BlockSpec((H_IDX, t_new, D_IDX), lambda b, c, pt: (0, b, 0)),
            pl.BlockSpec((t_new, H_IDX), lambda b, c, pt: (base + b, 0)),
            pl.BlockSpec((None, LANES, D_IDX), lambda b, c, pt: (b, 0, 0)),
        ] + _page_specs(l, (D_IDX, PAGE_SIZE), pps),
        out_specs=pl.BlockSpec((None, t_new, total), lambda b, c, pt: (b, 0, 0)),
        scratch_shapes=[pltpu.VMEM((D_IDX, pps * PAGE_SIZE), BF16),
                        pltpu.VMEM((t_new, total), I32),
                        pltpu.VMEM((t_new, SCORE_TILE), F32)],
    )
    return pl.pallas_call(
        kern, grid_spec=grid_spec,
        out_shape=jax.ShapeDtypeStruct((db, t_new, total), F32),
        compiler_params=_cparams(2), name="sample_indexer",
    )(page_table, qidx, widx, kidx_new_pad, *([cache] * pps))


def _samp_dsa_kernel(pt_ref, qa_ref, sel_ref, kvnew_ref, biasp_ref, biasn_ref, *rest, pps, past_len, t_new):
    pages = rest[:pps]
    out_ref, kbuf, m_ref, l_ref, acc_ref = rest[pps:]
    c = pl.program_id(1)
    n_steps = pl.num_programs(1)
    span = pps * PAGE_SIZE

    @pl.when(c == 0)
    def _():
        _softmax_init(m_ref, l_ref, acc_ref)

    def mask_add(sel):
        return jnp.concatenate([jnp.where(sel > 0.5, 0.0, NEG_BIG)] * H_A, axis=0)

    qa = qa_ref[...]
    q = jnp.concatenate([qa[:, h * DH_A:(h + 1) * DH_A] for h in range(H_A)], axis=0).astype(BF16)
    for p in range(pps):
        kbuf[p * PAGE_SIZE:(p + 1) * PAGE_SIZE, :] = pages[p][...].astype(BF16)
    off = pl.multiple_of(c * span, span)
    s = _dot_nt(q, kbuf[:, :DH_A]) + (biasp_ref[:, pl.ds(off, span)] + mask_add(sel_ref[:, pl.ds(off, span)]))
    _softmax_step(s, kbuf[:, DH_A:], m_ref, l_ref, acc_ref)

    @pl.when(c == n_steps - 1)
    def _():
        kvn = kvnew_ref[...]
        sn = _dot_nt(q, kvn[:, :DH_A]) + (biasn_ref[...] + mask_add(sel_ref[:, past_len:past_len + LANES]))
        _softmax_step(sn, kvn[:, DH_A:], m_ref, l_ref, acc_ref)
        o = acc_ref[...] / l_ref[...]
        for h in range(H_A):
            out_ref[:, h * DH_A:(h + 1) * DH_A] = o[h * t_new:(h + 1) * t_new, :]


def _samp_dsa(page_table, qa, sel, kva_new_pad, bias_past, bias_new, cache, l, t_new):
    db, n_pages = page_table.shape
    pps = min(PAGES_PER_STEP, n_pages)
    past_len = n_pages * PAGE_SIZE
    total = past_len + LANES
    rows = H_A * t_new
    kern = functools.partial(_samp_dsa_kernel, pps=pps, past_len=past_len, t_new=t_new)
    grid_spec = pltpu.PrefetchScalarGridSpec(
        num_scalar_prefetch=1,
        grid=(db, n_pages // pps),
        in_specs=[
            pl.BlockSpec((t_new, H_A * DH_A), lambda b, c, pt: (b, 0)),
            pl.BlockSpec((None, t_new, total), lambda b, c, pt: (b, 0, 0)),
            pl.BlockSpec((None, LANES, C_A), lambda b, c, pt: (b, 0, 0)),
            pl.BlockSpec((rows, past_len), lambda b, c, pt: (0, 0)),
            pl.BlockSpec((rows, LANES), lambda b, c, pt: (0, 0)),
        ] + _page_specs(l, (PAGE_SIZE, C_A), pps),
        out_specs=pl.BlockSpec((t_new, H_A * DH_A), lambda b, c, pt: (b, 0)),
        scratch_shapes=[
            pltpu.VMEM((pps * PAGE_SIZE, C_A), BF16),
            pltpu.VMEM((rows, LANES), F32),
            pltpu.VMEM((rows, LANES), F32),
            pltpu.VMEM((rows, DH_A), F32),
        ],
    )
    return pl.pallas_call(
        kern, grid_spec=grid_spec,
        out_shape=jax.ShapeDtypeStruct((db * t_new, H_A * DH_A), F32),
        compiler_params=_cparams(2), name="sample_dsa",
    )(page_table, qa, sel, kva_new_pad, bias_past, bias_new, *([cache] * pps))


def _samp_mla_kernel(pt_ref, q_ref, kvnew_ref, *rest, pps, t_new):
    pages = rest[:pps]
    out_ref, kbuf, m_ref, l_ref, acc_ref = rest[pps:]
    c = pl.program_id(1)
    n_steps = pl.num_programs(1)
    rows = H_B * t_new

    @pl.when(c == 0)
    def _():
        _softmax_init(m_ref, l_ref, acc_ref)

    q = q_ref[...].reshape(rows, C_B).astype(BF16)
    for p in range(pps):
        kbuf[:, p * PAGE_SIZE:(p + 1) * PAGE_SIZE] = pages[p][...].astype(BF16)
    _softmax_step(_dot(q, kbuf[...]), kbuf[0:KV_LORA, :], m_ref, l_ref, acc_ref, v_transposed=True)

    @pl.when(c == n_steps - 1)
    def _():
        kvn = kvnew_ref[...]
        qi = lax.broadcasted_iota(I32, (H_B, t_new, LANES), 1).reshape(rows, LANES)
        ki = lax.broadcasted_iota(I32, (rows, LANES), 1)
        sn = jnp.where(ki <= qi, _dot_nt(q, kvn), NEG_BIG)
        _softmax_step(sn, kvn[:, :KV_LORA], m_ref, l_ref, acc_ref)
        out_ref[...] = (acc_ref[...] / _lane_tile(l_ref[...], KV_LORA)).reshape(H_B, t_new, KV_LORA)


def _samp_mla(page_table, qcat, kvb_new_pad, cache, l, t_new):
    db, n_pages = page_table.shape
    pps = min(PAGES_PER_STEP, n_pages)
    rows = H_B * t_new
    kern = functools.partial(_samp_mla_kernel, pps=pps, t_new=t_new)
    grid_spec = pltpu.PrefetchScalarGridSpec(
        num_scalar_prefetch=1,
        grid=(db, n_pages // pps),
        in_specs=[
            pl.BlockSpec((H_B, t_new, C_B), lambda b, c, pt: (0, b, 0)),
            pl.BlockSpec((None, LANES, C_B), lambda b, c, pt: (b, 0, 0)),
        ] + _page_specs(l, (C_B, PAGE_SIZE), pps),
        out_specs=pl.BlockSpec((H_B, t_new, KV_LORA), lambda b, c, pt: (0, b, 0)),
        scratch_shapes=[
            pltpu.VMEM((C_B, pps * PAGE_SIZE), BF16),
            pltpu.VMEM((rows, LANES), F32),
            pltpu.VMEM((rows, LANES), F32),
            pltpu.VMEM((rows, KV_LORA), F32),
        ],
    )
    return pl.pallas_call(
        kern, grid_spec=grid_spec,
        out_shape=jax.ShapeDtypeStruct((H_B, db * t_new, KV_LORA), F32),
        compiler_params=_cparams(2), name="sample_mla",
    )(page_table, qcat, kvb_new_pad, *([cache] * pps))


def _merge_kernel(x_ref, a_ref, o_ref, wuv_ref, wout_ref, g_ref, b_ref, out_ref):
    width_a = H_A * DH_A
    mix = _dot(a_ref[...].astype(BF16), wout_ref[0:width_a, :])
    for h in range(H_B):
        b_h = _dot(o_ref[h].astype(BF16), wuv_ref[h]).astype(BF16)
        mix = mix + _dot(b_h, wout_ref[width_a + h * D_V:width_a + (h + 1) * D_V, :])
    out_ref[...] = _layer_norm(ALPHA * x_ref[...] + mix, g_ref[...], b_ref[...])


def _merge(x, a, o_lat, w, l):
    n = x.shape[0]
    tm = TOKEN_BLOCK
    row = lambda width: pl.BlockSpec((tm, width), lambda i: (i, 0))
    lay = lambda arr: pl.BlockSpec((None,) + arr.shape[1:], lambda i: (l,) + (0,) * (arr.ndim - 1))
    return pl.pallas_call(
        _merge_kernel,
        grid=(n // tm,),
        in_specs=[row(D_MODEL), row(H_A * DH_A), pl.BlockSpec((H_B, tm, KV_LORA), lambda i: (0, i, 0)),
                  lay(w["wuv"]), lay(w["wout"]), lay(w["ln1g"]), lay(w["ln1b"])],
        out_specs=row(D_MODEL),
        out_shape=jax.ShapeDtypeStruct((n, D_MODEL), F32),
        compiler_params=_cparams(1), name="merge_heads",
    )(x, a, o_lat, w["wuv"], w["wout"], w["ln1g"], w["ln1b"])


def _ffn_kernel(x_ref, wg_ref, wu_ref, wd_ref, g_ref, b_ref, out_ref, acc_ref):
    f = pl.program_id(1)

    @pl.when(f == 0)
    def _():
        acc_ref[...] = jnp.zeros(acc_ref.shape, F32)

    x = x_ref[...].astype(BF16)
    hid = jax.nn.silu(_dot(x, wg_ref[...])) * _dot(x, wu_ref[...])
    acc_ref[...] += _dot(hid.astype(BF16), wd_ref[...])

    @pl.when(f == pl.num_programs(1) - 1)
    def _():
        out_ref[...] = _layer_norm(ALPHA * x_ref[...] + acc_ref[...], g_ref[...], b_ref[...])


def _ff_chunk(d_ff):
    for parts in (2, 1, 11, 22):
        if d_ff % parts == 0 and (d_ff // parts) % LANES == 0:
            return d_ff // parts
    return d_ff


def _ffn_dense(x, w, i_dense, l):
    n = x.shape[0]
    tm = TOKEN_BLOCK
    d_ff = w["wg_d"].shape[-1]
    tf = _ff_chunk(d_ff)
    return pl.pallas_call(
        _ffn_kernel,
        grid=(n // tm, d_ff // tf),
        in_specs=[pl.BlockSpec((tm, D_MODEL), lambda i, f: (i, 0)),
                  pl.BlockSpec((None, D_MODEL, tf), lambda i, f: (i_dense, 0, f)),
                  pl.BlockSpec((None, D_MODEL, tf), lambda i, f: (i_dense, 0, f)),
                  pl.BlockSpec((None, tf, D_MODEL), lambda i, f: (i_dense, f, 0)),
                  pl.BlockSpec((None, 1, D_MODEL), lambda i, f: (l, 0, 0)),
                  pl.BlockSpec((None, 1, D_MODEL), lambda i, f: (l, 0, 0))],
        out_specs=pl.BlockSpec((tm, D_MODEL), lambda i, f: (i, 0)),
        out_shape=jax.ShapeDtypeStruct((n, D_MODEL), F32),
        scratch_shapes=[pltpu.VMEM((tm, D_MODEL), F32)],
        compiler_params=_cparams(2), name="ffn_dense",
    )(x, w["wg_d"], w["wu_d"], w["wd_d"], w["ln2g"], w["ln2b"])


def _router_kernel(x_ref, wr_ref, br_ref, gates_ref):
    logits = jnp.dot(x_ref[...], wr_ref[...], preferred_element_type=F32,
                     precision=lax.Precision.HIGHEST) + br_ref[...]
    lane = lax.broadcasted_iota(I32, logits.shape, 1).astype(F32)
    logits = jnp.where(lane < N_EXPERTS, logits, -jnp.inf)
    m1 = jnp.max(logits, axis=1, keepdims=True)
    i1 = jnp.min(jnp.where(logits == m1, lane, float(LANES)), axis=1, keepdims=True)
    rest = jnp.where(lane == i1, -jnp.inf, logits)
    m2 = jnp.max(rest, axis=1, keepdims=True)
    i2 = jnp.min(jnp.where(rest == m2, lane, float(LANES)), axis=1, keepdims=True)
    e2 = jnp.exp(m2 - m1)
    g1 = 1.0 / (1.0 + e2)
    g2 = e2 / (1.0 + e2)
    for e in range(N_EXPERTS):
        gate = jnp.where(i1 == e, g1, 0.0) + jnp.where(i2 == e, g2, 0.0)
        gates_ref[e] = jnp.broadcast_to(gate, (gate.shape[0], LANES))


def _router(x, w, i_moe):
    n = x.shape[0]
    tm = TOKEN_BLOCK
    return pl.pallas_call(
        _router_kernel,
        grid=(n // tm,),
        in_specs=[pl.BlockSpec((tm, D_MODEL), lambda i: (i, 0)),
                  pl.BlockSpec((None, D_MODEL, LANES), lambda i: (i_moe, 0, 0)),
                  pl.BlockSpec((None, 1, LANES), lambda i: (i_moe, 0, 0))],
        out_specs=pl.BlockSpec((N_EXPERTS, tm, LANES), lambda i: (0, i, 0)),
        out_shape=jax.ShapeDtypeStruct((N_EXPERTS, n, LANES), F32),
        compiler_params=_cparams(1), name="moe_router",
    )(x, w["wr"], w["br"])


def _moe_kernel(x_ref, gate_ref, wg_ref, wu_ref, wd_ref, g_ref, b_ref, out_ref, acc_ref):
    e = pl.program_id(1)
    f = pl.program_id(2)

    @pl.when((e == 0) & (f == 0))
    def _():
        acc_ref[...] = jnp.zeros(acc_ref.shape, F32)

    x = x_ref[...].astype(BF16)
    hid = jax.nn.silu(_dot(x, wg_ref[...])) * _dot(x, wu_ref[...])
    y = _dot(hid.astype(BF16), wd_ref[...])
    gate = jnp.concatenate([gate_ref[...]] * (D_MODEL // LANES), axis=1)
    acc_ref[...] += gate * y

    @pl.when((e == pl.num_programs(1) - 1) & (f == pl.num_programs(2) - 1))
    def _():
        out_ref[...] = _layer_norm(ALPHA * x_ref[...] + acc_ref[...], g_ref[...], b_ref[...])


def _ffn_moe(x, gates, w, i_moe, l):
    n = x.shape[0]
    tm = TOKEN_BLOCK
    d_ff = w["wg_e"].shape[-1]
    tf = _ff_chunk(d_ff)
    return pl.pallas_call(
        _moe_kernel,
        grid=(n // tm, N_EXPERTS, d_ff // tf),
        in_specs=[pl.BlockSpec((tm, D_MODEL), lambda i, e, f: (i, 0)),
                  pl.BlockSpec((None, tm, LANES), lambda i, e, f: (e, i, 0)),
                  pl.BlockSpec((None, None, D_MODEL, tf), lambda i, e, f: (i_moe, e, 0, f)),
                  pl.BlockSpec((None, None, D_MODEL, tf), lambda i, e, f: (i_moe, e, 0, f)),
                  pl.BlockSpec((None, None, tf, D_MODEL), lambda i, e, f: (i_moe, e, f, 0)),
                  pl.BlockSpec((None, 1, D_MODEL), lambda i, e, f: (l, 0, 0)),
                  pl.BlockSpec((None, 1, D_MODEL), lambda i, e, f: (l, 0, 0))],
        out_specs=pl.BlockSpec((tm, D_MODEL), lambda i, e, f: (i, 0)),
        out_shape=jax.ShapeDtypeStruct((n, D_MODEL), F32),
        scratch_shapes=[pltpu.VMEM((tm, D_MODEL), F32)],
        compiler_params=_cparams(3), name="ffn_moe",
    )(x, gates, w["wg_e"], w["wu_e"], w["wd_e"], w["ln2g"], w["ln2b"])


def _swap_halves(w):
    half = w.shape[-1] // 2
    return jnp.concatenate([w[..., half:], w[..., :half]], axis=-1)


def _prepare_weights(w_in, idx_ln_g, idx_ln_b, q_norm_g, w_uq, kv_norm_g, w_uk, w_uv, w_out, ln1_g, ln1_b,
                     ln2_g, ln2_b, w_gate_dense, w_up_dense, w_down_dense, w_router, b_router,
                     w_gate_exp, w_up_exp, w_down_exp):
    bf = lambda a: a.astype(BF16)
    depth = w_in.shape[0]
    splits = (H_A * DH_A, DH_A, DH_A, H_IDX * D_IDX, D_IDX, H_IDX, Q_LORA, KV_LORA, D_ROPE)
    cols, start = [], 0
    for width in splits:
        cols.append(w_in[:, :, start:start + width])
        start += width
    w_qa, w_ka, w_va, w_qidx, w_kidx, w_widx, w_cq, w_ckv, w_kr = cols
    w_uq4 = w_uq.reshape(depth, Q_LORA, H_B, D_NOPE + D_ROPE)
    w_uq_rope = w_uq4[..., D_NOPE:]
    vec = lambda a: a[:, None, :]
    return {
        "wqa": bf(w_qa),
        "wkva": bf(jnp.concatenate([w_ka, w_va], axis=-1)),
        "wqidx": bf(w_qidx.reshape(depth, D_MODEL, H_IDX, D_IDX).transpose(0, 2, 1, 3)),
        "wkidx": bf(w_kidx),
        "wwidx": bf(jnp.pad(w_widx, ((0, 0), (0, 0), (0, LANES - H_IDX)))),
        "wcq": bf(w_cq),
        "wckv": bf(w_ckv),
        "wkr": bf(jnp.stack([w_kr, _swap_halves(w_kr)], axis=1)),
        "idxg": vec(idx_ln_g), "idxb": vec(idx_ln_b), "qng": vec(q_norm_g), "kvng": vec(kv_norm_g),
        "wuqn": bf(w_uq4[..., :D_NOPE].transpose(0, 2, 1, 3)),
        "wuqr": bf(w_uq_rope.transpose(0, 2, 1, 3)),
        "wuqrs": bf(_swap_halves(w_uq_rope).transpose(0, 2, 1, 3)),
        "wukt": bf(w_uk.transpose(0, 2, 3, 1)),
        "wuv": bf(w_uv.transpose(0, 2, 1, 3)),
        "wout": bf(w_out),
        "ln1g": vec(ln1_g), "ln1b": vec(ln1_b), "ln2g": vec(ln2_g), "ln2b": vec(ln2_b),
        "wg_d": bf(w_gate_dense), "wu_d": bf(w_up_dense), "wd_d": bf(w_down_dense),
        "wr": jnp.pad(w_router, ((0, 0), (0, 0), (0, LANES - N_EXPERTS))),
        "br": jnp.pad(b_router, ((0, 0), (0, LANES - N_EXPERTS)))[:, None, :],
        "wg_e": bf(w_gate_exp), "wu_e": bf(w_up_exp), "wd_e": bf(w_down_exp),
    }


def _rope_tables(pos):
    half = D_ROPE // 2
    inv = ROPE_THETA ** (-jnp.arange(half, dtype=F32) / half)
    ang = pos.astype(F32)[:, None] * inv[None, :]
    cos, sin = jnp.cos(ang), jnp.sin(ang)
    return jnp.concatenate([cos, cos], axis=-1), jnp.concatenate([-sin, sin], axis=-1)


def _pad_new(a, db, t_new):
    a = a.reshape(db, t_new, a.shape[-1])
    return jnp.pad(a, ((0, 0), (0, LANES - t_new), (0, 0)))


def kernel(x_prompt, x_sample, cache_a_kv, cache_a_kidx, cache_b_latent, page_table, rel_bias, w_in, idx_ln_g, idx_ln_b, q_norm_g, w_uq, kv_norm_g, w_uk, w_uv, w_out, ln1_g, ln1_b, ln2_g, ln2_b, w_gate_dense, w_up_dense, w_down_dense, w_router, b_router, w_gate_exp, w_up_exp, w_down_exp):
    batch, seq, _ = x_prompt.shape
    db, t_new, _ = x_sample.shape
    depth = w_in.shape[0]
    n_prompt = batch * seq
    n_sample = db * t_new
    n = n_prompt + n_sample
    past_len = page_table.shape[1] * PAGE_SIZE
    assert seq % KEY_CHUNK == 0 and seq % MLA_KEY_CHUNK == 0 and n % TOKEN_BLOCK == 0 and n_prompt % t_new == 0 and t_new % 8 == 0
    assert page_table.shape[1] % min(PAGES_PER_STEP, page_table.shape[1]) == 0

    w = _prepare_weights(w_in, idx_ln_g, idx_ln_b, q_norm_g, w_uq, kv_norm_g, w_uk, w_uv, w_out, ln1_g, ln1_b,
                         ln2_g, ln2_b, w_gate_dense, w_up_dense, w_down_dense, w_router, b_router,
                         w_gate_exp, w_up_exp, w_down_exp)
    pos = jnp.concatenate([jnp.tile(jnp.arange(seq, dtype=I32), batch),
                           jnp.tile(past_len + jnp.arange(t_new, dtype=I32), db)])
    cos_t, sin_t = _rope_tables(pos)
    bias_tile, bias_past, bias_new = _bias_tables(rel_bias, past_len, t_new)

    cache_kidx_t = jnp.swapaxes(cache_a_kidx, 2, 3)
    cache_lat_t = jnp.swapaxes(cache_b_latent, 2, 3)

    x = jnp.concatenate([x_prompt.reshape(n_prompt, D_MODEL), x_sample.reshape(n_sample, D_MODEL)], axis=0)
    kva_all, kidx_all, kvb_all = [], [], []
    for l in range(depth):
        qa, kva, kva_bf, qidx, kidx, kidx_bf, widx, qcat, kvb, kvb_bf = _projections(x, cos_t, sin_t, w, l)
        a_p = _dsa_prompt(qa, qidx, widx, kidx_bf, kva_bf, bias_tile, batch, seq)
        o_p = _mla_prompt(qcat, kvb_bf, batch, seq)
        qidx_s = qidx[:, n_prompt:].astype(F32)
        qa_s = qa[n_prompt:].astype(F32)
        qcat_s = qcat[:, n_prompt:].astype(F32)
        sel = _samp_idx(page_table, qidx_s, widx, _pad_new(kidx_bf[n_prompt:], db, t_new), cache_kidx_t, l,
                        n_prompt, t_new)
        a_s = _samp_dsa(page_table, qa_s, sel, _pad_new(kva_bf[n_prompt:], db, t_new), bias_past, bias_new,
                        cache_a_kv, l, t_new)
        o_s = _samp_mla(page_table, qcat_s, _pad_new(kvb_bf[n_prompt:], db, t_new), cache_lat_t, l, t_new)
        a = jnp.concatenate([a_p, a_s], axis=0)
        o_lat = jnp.concatenate([o_p, o_s], axis=1)
        x = _merge(x, a, o_lat, w, l)
        if l % 2 == 0:
            x = _ffn_dense(x, w, l // 2, l)
        else:
            x = _ffn_moe(x, _router(x, w, l // 2), w, l // 2, l)
        kva_all.append(kva)
        kidx_all.append(kidx)
        kvb_all.append(kvb)

    def split(parts, width):
        full = jnp.stack(parts)
        return (full[:, :n_prompt].reshape(depth, batch, seq, width),
                full[:, n_prompt:].reshape(depth, db, t_new, width))

    kva_p, kva_s = split(kva_all, C_A)
    kidx_p, kidx_s = split(kidx_all, D_IDX)
    kvb_p, kvb_s = split(kvb_all, C_B)
    return (x[:n_prompt].reshape(batch, seq, D_MODEL), x[n_prompt:].reshape(db, t_new, D_MODEL),
            kva_p, kidx_p, kvb_p, kva_s, kidx_s, kvb_s)
```

```python
import functools
import math

import jax
import jax.numpy as jnp
from jax import lax
from jax.experimental import pallas as pl
from jax.experimental.pallas import tpu as pltpu

F32 = jnp.float32
BF16 = jnp.bfloat16
I32 = jnp.int32

D_MODEL = 1024
DEPTH = 4
PAGE_SIZE = 128
H_A = 4
DH_A = 128
H_IDX = 8
D_IDX = 64
TOPK_MAX = 256
N_BUCKETS = 32
T5_MAX_DIST = 128
H_B = 4
D_NOPE = 128
D_ROPE = 64
D_V = 128
Q_LORA = 384
KV_LORA = 256
ROPE_THETA = 10000.0
C_A = 2 * DH_A
C_B = KV_LORA + D_ROPE
N_EXPERTS = 8
ALPHA = (2 * DEPTH) ** 0.25
EPS = 1e-5
Q_BLOCK = 128
MLA_SCALE = (D_NOPE + D_ROPE) ** -0.5
LOG2E = math.log2(math.e)
QA_SCALE = DH_A ** -0.5 * LOG2E
QCAT_SCALE = MLA_SCALE * LOG2E

LANES = 128
TOKEN_BLOCK = 512
KEY_CHUNK = 512
SCORE_TILE = 256
BIAS_TILE = 256
MLA_KEY_CHUNK = 512
PAGES_PER_STEP = 64
VMEM_LIMIT = 56 * 1024 * 1024

assert 3 * Q_BLOCK - (BIAS_TILE - 1) >= T5_MAX_DIST and KEY_CHUNK % BIAS_TILE == 0 and KEY_CHUNK % SCORE_TILE == 0

INT_MIN = -(2 ** 31)
NEG_BIG = -1e30
NT_DIMS = (((1,), (1,)), ((), ()))


def _cparams(n_axes):
    return pltpu.CompilerParams(dimension_semantics=("arbitrary",) * n_axes,
                                vmem_limit_bytes=VMEM_LIMIT)


def _dot(a, b):
    return jnp.dot(a, b, preferred_element_type=F32)


def _dot_nt(a, b):
    return lax.dot_general(a, b, NT_DIMS, preferred_element_type=F32)


def _layer_norm(x, g, b):
    xc = x - jnp.mean(x, axis=-1, keepdims=True)
    y = xc * lax.rsqrt(jnp.mean(xc * xc, axis=-1, keepdims=True) + EPS)
    return y * g + b


def _rms_norm(x, g):
    return x * lax.rsqrt(jnp.mean(x * x, axis=-1, keepdims=True) + EPS) * g


def _order_key(score):
    bits = lax.bitcast_convert_type(score, I32)
    return jnp.where(bits < 0, bits ^ jnp.int32(0x7FFFFFFF), bits)


def _bias_of_dist(dist, relb_ref, h):
    n = jnp.maximum(dist, 0)
    max_exact = N_BUCKETS // 2
    nf = jnp.maximum(n, max_exact).astype(F32)
    large = max_exact + (jnp.log(nf / max_exact) / math.log(T5_MAX_DIST / max_exact)
                         * (N_BUCKETS - max_exact)).astype(I32)
    bucket = jnp.where(n < max_exact, n, jnp.minimum(large, N_BUCKETS - 1))
    out = jnp.zeros(dist.shape, F32)
    for b in range(N_BUCKETS):
        out = jnp.where(bucket == b, relb_ref[b, h], out)
    return out * LOG2E


def _bias_tables_kernel(relb_ref, tile_ref, past_ref, new_ref, *, past_len, t_new):
    r = lax.broadcasted_iota(I32, (Q_BLOCK, BIAS_TILE), 0)
    c = lax.broadcasted_iota(I32, (Q_BLOCK, BIAS_TILE), 1)
    for m in range(4):
        for h in range(H_A):
            tile_ref[m, h] = _bias_of_dist(Q_BLOCK * m + r - c, relb_ref, h) - relb_ref[N_BUCKETS - 1, h] * LOG2E
    q = lax.broadcasted_iota(I32, (t_new, past_len), 0)
    s = lax.broadcasted_iota(I32, (t_new, past_len), 1)
    qn = lax.broadcasted_iota(I32, (t_new, LANES), 0)
    sn = lax.broadcasted_iota(I32, (t_new, LANES), 1)
    for h in range(H_A):
        past_ref[h * t_new:(h + 1) * t_new, :] = _bias_of_dist(past_len + q - s, relb_ref, h)
        new_ref[h * t_new:(h + 1) * t_new, :] = _bias_of_dist(qn - sn, relb_ref, h)


def _bias_tables(rel_bias, past_len, t_new):
    return pl.pallas_call(
        functools.partial(_bias_tables_kernel, past_len=past_len, t_new=t_new),
        out_shape=(jax.ShapeDtypeStruct((4, H_A, Q_BLOCK, BIAS_TILE), F32),
                   jax.ShapeDtypeStruct((H_A * t_new, past_len), F32),
                   jax.ShapeDtypeStruct((H_A * t_new, LANES), F32)),
        in_specs=[pl.BlockSpec(memory_space=pltpu.SMEM)],
        name="bias_tables",
    )(rel_bias)


def _proj_kernel(x_ref, cos_ref, sin_ref, wqa, wkva, wqidx, wkidx, wwidx, wcq, wckv, wkr,
                 idxg, idxb, qng, kvng, wuqn, wuqr, wuqrs, wukt,
                 qa_o, kva_o, kvabf_o, qidx_o, kidx_o, kidxbf_o, widx_o, qcat_o, kvb_o, kvbbf_o):
    x = x_ref[...].astype(BF16)
    cos = cos_ref[...]
    sin = sin_ref[...]
    qa_o[...] = (_dot(x, wqa[...]) * QA_SCALE).astype(BF16)
    kva = _dot(x, wkva[...])
    kva_o[...] = kva
    kvabf_o[...] = kva.astype(BF16)
    for h in range(H_IDX):
        qidx_o[h] = (_dot(x, wqidx[h]) * (D_IDX ** -0.5)).astype(BF16)
    kidx = _layer_norm(_dot(x, wkidx[...]), idxg[...], idxb[...])
    kidx_o[...] = kidx
    kidxbf_o[...] = kidx.astype(BF16)
    widx_o[...] = _dot(x, wwidx[...])[:, :H_IDX] * (H_IDX ** -0.5)
    cq = _rms_norm(_dot(x, wcq[...]), qng[...]).astype(BF16)
    for h in range(H_B):
        q_nope = _dot(cq, wuqn[h]).astype(BF16)
        q_lat = _dot(q_nope, wukt[h])
        q_rope = _dot(cq, wuqr[h]) * cos + _dot(cq, wuqrs[h]) * sin
        qcat_o[h, :, 0:KV_LORA] = (q_lat * QCAT_SCALE).astype(BF16)
        qcat_o[h, :, KV_LORA:C_B] = (q_rope * QCAT_SCALE).astype(BF16)
    ckv = _rms_norm(_dot(x, wckv[...]), kvng[...])
    k_rope = _dot(x, wkr[0]) * cos + _dot(x, wkr[1]) * sin
    kvb_o[:, 0:KV_LORA] = ckv
    kvb_o[:, KV_LORA:C_B] = k_rope
    kvbbf_o[:, 0:KV_LORA] = ckv.astype(BF16)
    kvbbf_o[:, KV_LORA:C_B] = k_rope.astype(BF16)


def _projections(x, cos_t, sin_t, w, l):
    n = x.shape[0]
    tm = TOKEN_BLOCK
    row = lambda width: pl.BlockSpec((tm, width), lambda i: (i, 0))
    lay2 = lambda a: pl.BlockSpec((None,) + a.shape[1:], lambda i: (l,) + (0,) * (a.ndim - 1))
    weights = [w["wqa"], w["wkva"], w["wqidx"], w["wkidx"], w["wwidx"], w["wcq"], w["wckv"], w["wkr"],
               w["idxg"], w["idxb"], w["qng"], w["kvng"], w["wuqn"], w["wuqr"], w["wuqrs"], w["wukt"]]
    out_shape = (
        jax.ShapeDtypeStruct((n, H_A * DH_A), BF16),
        jax.ShapeDtypeStruct((n, C_A), F32),
        jax.ShapeDtypeStruct((n, C_A), BF16),
        jax.ShapeDtypeStruct((H_IDX, n, D_IDX), BF16),
        jax.ShapeDtypeStruct((n, D_IDX), F32),
        jax.ShapeDtypeStruct((n, D_IDX), BF16),
        jax.ShapeDtypeStruct((n, H_IDX), F32),
        jax.ShapeDtypeStruct((H_B, n, C_B), BF16),
        jax.ShapeDtypeStruct((n, C_B), F32),
        jax.ShapeDtypeStruct((n, C_B), BF16),
    )
    out_specs = (row(H_A * DH_A), row(C_A), row(C_A),
                 pl.BlockSpec((H_IDX, tm, D_IDX), lambda i: (0, i, 0)),
                 row(D_IDX), row(D_IDX), row(H_IDX),
                 pl.BlockSpec((H_B, tm, C_B), lambda i: (0, i, 0)),
                 row(C_B), row(C_B))
    return pl.pallas_call(
        _proj_kernel,
        grid=(n // tm,),
        in_specs=[row(D_MODEL), row(D_ROPE), row(D_ROPE)] + [lay2(a) for a in weights],
        out_specs=out_specs,
        out_shape=out_shape,
        compiler_params=_cparams(1),
        name="projections",
    )(x, cos_t, sin_t, *weights)


def _lane_tile(x, width):
    reps = width // LANES
    return x if reps == 1 else jnp.concatenate([x] * reps, axis=1)


def _softmax_init(m_ref, l_ref, acc_ref):
    m_ref[...] = jnp.full(m_ref.shape, NEG_BIG, F32)
    l_ref[...] = jnp.zeros(l_ref.shape, F32)
    acc_ref[...] = jnp.zeros(acc_ref.shape, F32)


def _softmax_step(s, v, m_ref, l_ref, acc_ref, v_transposed=False):
    m_old = m_ref[...]
    m_new = jnp.maximum(m_old, jnp.max(s, axis=1, keepdims=True))
    p = jnp.exp2(s - _lane_tile(m_new, s.shape[1]))
    alpha = jnp.exp2(m_old - m_new)
    l_ref[...] = alpha * l_ref[...] + jnp.sum(p, axis=1, keepdims=True)
    pv = _dot_nt(p.astype(BF16), v) if v_transposed else _dot(p.astype(BF16), v)
    acc_ref[...] = _lane_tile(alpha, acc_ref.shape[1]) * acc_ref[...] + pv
    m_ref[...] = m_new


def _select_threshold(keys_ref, n_rows, n_chunks, chunk, n_keep, col_bits, bracket=None):
    whole = n_chunks is None
    lane_iota = lax.broadcasted_iota(I32, (n_rows, LANES), 1)

    def count(pred):
        if whole:
            ind = [jnp.where(pred(keys_ref[:, g * LANES:(g + 1) * LANES], g * LANES), 1.0, 0.0)
                   for g in range(chunk // LANES)]
            while len(ind) > 1:
                ind = [a + b for a, b in zip(ind[0::2], ind[1::2])] + ([ind[-1]] if len(ind) % 2 else [])
            return jnp.sum(ind[0], axis=1, keepdims=True)

        def body(j, c):
            off = pl.multiple_of(j * chunk, chunk)
            for g in range(chunk // LANES):
                col0 = off + g * LANES
                c = c + jnp.where(pred(keys_ref[:, pl.ds(pl.multiple_of(col0, LANES), LANES)], col0), 1.0, 0.0)
            return c
        c = lax.fori_loop(0, n_chunks, body, jnp.zeros((n_rows, LANES), F32))
        return jnp.sum(c, axis=1, keepdims=True)

    def lanes(col):
        return jnp.broadcast_to(col, (n_rows, LANES))

    if bracket is None:
        def bit_body(p, prefix):
            cand = prefix + lax.shift_left(jnp.int32(1), 31 - p)
            cand_b = lanes(cand)
            cnt = count(lambda t, col0: t >= cand_b)
            return jnp.where(cnt >= n_keep, cand, prefix)

        thr = lax.fori_loop(0, 32, bit_body, jnp.full((n_rows, 1), INT_MIN, I32))
        unsettled = None
    else:
        def probe(state):
            lo, hi, thr, done, _ = state
            mid = (lo >> 1) + (hi >> 1) + (lo & hi & 1)
            mid_b = lanes(mid)
            cnt = count(lambda t, col0: t >= mid_b)
            enough = cnt >= n_keep
            exact = cnt == n_keep
            thr = jnp.where(exact & (done == 0), mid, thr)
            done = jnp.where(exact, 1, done)
            lo = jnp.where(enough, mid, lo)
            hi = jnp.where(enough, hi, mid)
            still = jnp.max(jnp.where((done == 0) & (hi > lo + 1), 1.0, 0.0))
            return lo, hi, thr, done, still

        lo0, hi0 = bracket
        state = (lo0, hi0, lo0, jnp.zeros((n_rows, 1), I32), jnp.max(jnp.where(hi0 > lo0 + 1, 1.0, 0.0)))
        lo, _, thr, done, _ = lax.while_loop(lambda state: state[4] > 0.0, probe, state)
        thr = jnp.where(done == 1, thr, lo)
        unsettled = jnp.max(jnp.where(done == 0, 1.0, 0.0)) > 0.0
    thr = jnp.maximum(thr, INT_MIN + 1)
    thr_b = lanes(thr)

    def resolve_ties():
        n_gt = count(lambda t, col0: t > thr_b)
        n_eq = count(lambda t, col0: t == thr_b)
        need = n_keep - n_gt
        tie = jnp.max(jnp.where(n_eq > need, 1.0, 0.0)) > 0.0

        @pl.when(tie)
        def _():
            def cut_body(p, cut):
                cand = cut + lax.shift_left(jnp.int32(1), col_bits - 1 - p)
                cand_b = lanes(cand)
                cnt = count(lambda t, col0: (t == thr_b) & (col0 + lane_iota < cand_b))
                return jnp.where(cnt < need, cand, cut)

            cut_b = lanes(lax.fori_loop(0, col_bits, cut_body, jnp.zeros((n_rows, 1), I32)))

            def demote_tile(col0):
                t = keys_ref[:, pl.ds(col0, LANES)]
                lose = (t == thr_b) & (col0 + lane_iota > cut_b)
                keys_ref[:, pl.ds(col0, LANES)] = jnp.where(lose, thr_b - 1, t)

            if whole:
                for g in range(chunk // LANES):
                    demote_tile(g * LANES)
            else:
                def demote(j, carry):
                    for g in range(chunk // LANES):
                        demote_tile(pl.multiple_of(j * chunk + g * LANES, LANES))
                    return carry
                lax.fori_loop(0, n_chunks, demote, 0)

    if unsettled is None:
        resolve_ties()
    else:
        pl.when(unsettled)(resolve_ties)
    return thr


def _dsa_prompt_kernel(qidx_ref, widx_ref, qa_ref, kidx_ref, kva_ref, bias_ref, out_ref,
                       keys_ref, wb_ref, gmax_ref, qs_ref, m_ref, l_ref, acc_ref, *, n_keep, col_bits):
    i = pl.program_id(1)
    kc = KEY_CHUNK
    st = SCORE_TILE
    n_chunks = i // (kc // Q_BLOCK) + 1
    row_pos = i * Q_BLOCK + lax.broadcasted_iota(I32, (Q_BLOCK, st), 0)
    lane = lax.broadcasted_iota(I32, (Q_BLOCK, st), 1)

    for h in range(H_IDX):
        wb_ref[h] = jnp.broadcast_to(widx_ref[:, h:h + 1], (Q_BLOCK, st))
    gmax_ref[...] = jnp.full(gmax_ref.shape, -jnp.inf, F32)

    def score_body(j, carry):
        for sub in range(kc // st):
            off = pl.multiple_of(j * kc + sub * st, st)
            keys_c = kidx_ref[pl.ds(off, st), :]
            score = jnp.zeros((Q_BLOCK, st), F32)
            for h in range(H_IDX):
                score = score + wb_ref[h] * jnp.maximum(_dot_nt(qidx_ref[h], keys_c), 0.0)
            visible = off + lane <= row_pos
            keys_ref[:, pl.ds(off, st)] = jnp.where(visible, _order_key(score), INT_MIN)
            gmax_ref[...] = jnp.maximum(gmax_ref[...], jnp.where(visible, score, -jnp.inf))
        return carry

    lax.fori_loop(0, n_chunks, score_body, 0)
    gmax = gmax_ref[...]
    lo0 = _order_key(jnp.min(gmax, axis=1, keepdims=True))
    hi0 = _order_key(jnp.max(gmax, axis=1, keepdims=True)) + 1
    thr = _select_threshold(keys_ref, Q_BLOCK, n_chunks, kc, n_keep, col_bits, bracket=(lo0, hi0))

    rows = H_A * Q_BLOCK
    _softmax_init(m_ref, l_ref, acc_ref)
    for h in range(H_A):
        qs_ref[h * Q_BLOCK:(h + 1) * Q_BLOCK, :] = qa_ref[:, h * DH_A:(h + 1) * DH_A]
    thr_b = jnp.broadcast_to(thr, (Q_BLOCK, kc))
    tiles_per_chunk = kc // BIAS_TILE
    n_far = jnp.maximum(n_chunks - 2, 0)

    def logits(j):
        return _dot_nt(qs_ref[...], kva_ref[pl.ds(pl.multiple_of(j * kc, kc), kc), 0:DH_A])

    def attend(j, s_raw, near):
        s_next = logits(jnp.minimum(j + 1, n_chunks - 1))
        off = pl.multiple_of(j * kc, kc)
        add = jnp.where(keys_ref[:, pl.ds(off, kc)] >= thr_b, 0.0, NEG_BIG)[None]
        if near:
            m = i - (kc // Q_BLOCK) * j
            tiles = [bias_ref[jnp.clip(m - (BIAS_TILE // Q_BLOCK) * t, 0, 3)] for t in range(tiles_per_chunk)]
            add = add + jnp.concatenate(tiles, axis=2)
        s = (s_raw.reshape(H_A, Q_BLOCK, kc) + add).reshape(rows, kc)
        _softmax_step(s, kva_ref[pl.ds(off, kc), DH_A:C_A], m_ref, l_ref, acc_ref)
        return s_next

    s_raw = lax.fori_loop(0, n_far, lambda j, s: attend(j, s, False), logits(0))
    lax.fori_loop(n_far, n_chunks, lambda j, s: attend(j, s, True), s_raw)
    o = acc_ref[...] / l_ref[...]
    for h in range(H_A):
        out_ref[:, h * DH_A:(h + 1) * DH_A] = o[h * Q_BLOCK:(h + 1) * Q_BLOCK, :]


def _dsa_prompt(qa, qidx, widx, kidx_bf, kva_bf, bias_tile, batch, seq):
    nq = seq // Q_BLOCK
    n_keep = min(TOPK_MAX, seq // 4)
    assert n_keep <= SCORE_TILE and DH_A == LANES
    kern = functools.partial(_dsa_prompt_kernel, n_keep=n_keep, col_bits=max(1, (seq - 1).bit_length()))
    return pl.pallas_call(
        kern,
        grid=(batch, nq),
        in_specs=[
            pl.BlockSpec((H_IDX, Q_BLOCK, D_IDX), lambda b, i: (0, b * nq + i, 0)),
            pl.BlockSpec((Q_BLOCK, H_IDX), lambda b, i: (b * nq + i, 0)),
            pl.BlockSpec((Q_BLOCK, H_A * DH_A), lambda b, i: (b * nq + i, 0)),
            pl.BlockSpec((seq, D_IDX), lambda b, i: (b, 0)),
            pl.BlockSpec((seq, C_A), lambda b, i: (b, 0)),
            pl.BlockSpec(bias_tile.shape, lambda b, i: (0, 0, 0, 0)),
        ],
        out_specs=pl.BlockSpec((Q_BLOCK, H_A * DH_A), lambda b, i: (b * nq + i, 0)),
        out_shape=jax.ShapeDtypeStruct((batch * seq, H_A * DH_A), F32),
        scratch_shapes=[
            pltpu.VMEM((Q_BLOCK, seq), I32),
            pltpu.VMEM((H_IDX, Q_BLOCK, SCORE_TILE), F32),
            pltpu.VMEM((Q_BLOCK, SCORE_TILE), F32),
            pltpu.VMEM((H_A * Q_BLOCK, DH_A), BF16),
            pltpu.VMEM((H_A * Q_BLOCK, LANES), F32),
            pltpu.VMEM((H_A * Q_BLOCK, LANES), F32),
            pltpu.VMEM((H_A * Q_BLOCK, DH_A), F32),
        ],
        compiler_params=_cparams(2),
        name="dsa_prompt",
    )(qidx, widx, qa, kidx_bf, kva_bf, bias_tile)


def _mla_prompt_kernel(q_ref, kvb_ref, out_ref, m_ref, l_ref, acc_ref):
    i = pl.program_id(1)
    kc = MLA_KEY_CHUNK
    rows = H_B * Q_BLOCK
    n_full = i // (kc // Q_BLOCK)
    q = q_ref[...].reshape(rows, C_B)
    _softmax_init(m_ref, l_ref, acc_ref)

    def keys(j):
        return kvb_ref[pl.ds(pl.multiple_of(j * kc, kc), kc), :]

    def body(j, s):
        s_next = _dot_nt(q, keys(j + 1))
        _softmax_step(s, keys(j)[:, :KV_LORA], m_ref, l_ref, acc_ref)
        return s_next

    s_last = lax.fori_loop(0, n_full, body, _dot_nt(q, keys(0)))
    row_pos = i * Q_BLOCK + lax.broadcasted_iota(I32, (H_B, Q_BLOCK, kc), 1).reshape(rows, kc)
    visible = n_full * kc + lax.broadcasted_iota(I32, (rows, kc), 1) <= row_pos
    s_last = jnp.where(visible, s_last, NEG_BIG)
    _softmax_step(s_last, keys(n_full)[:, :KV_LORA], m_ref, l_ref, acc_ref)
    out_ref[...] = (acc_ref[...] / _lane_tile(l_ref[...], KV_LORA)).reshape(H_B, Q_BLOCK, KV_LORA)


def _mla_prompt(qcat, kvb_bf, batch, seq):
    nq = seq // Q_BLOCK
    rows = H_B * Q_BLOCK
    return pl.pallas_call(
        _mla_prompt_kernel,
        grid=(batch, nq),
        in_specs=[
            pl.BlockSpec((H_B, Q_BLOCK, C_B), lambda b, i: (0, b * nq + i, 0)),
            pl.BlockSpec((seq, C_B), lambda b, i: (b, 0)),
        ],
        out_specs=pl.BlockSpec((H_B, Q_BLOCK, KV_LORA), lambda b, i: (0, b * nq + i, 0)),
        out_shape=jax.ShapeDtypeStruct((H_B, batch * seq, KV_LORA), F32),
        scratch_shapes=[
            pltpu.VMEM((rows, LANES), F32),
            pltpu.VMEM((rows, LANES), F32),
            pltpu.VMEM((rows, KV_LORA), F32),
        ],
        compiler_params=_cparams(2),
        name="mla_prompt",
    )(qcat, kvb_bf)


def _page_specs(l, page_shape, pps):
    def spec(p):
        return pl.BlockSpec((None, None) + page_shape,
                            lambda b, c, pt: (l, pt[b, c * pps + p], 0, 0))
    return [spec(p) for p in range(pps)]


def _samp_idx_kernel(pt_ref, qidx_ref, widx_ref, knew_ref, *rest, pps, past_len, t_new, n_keep, col_bits):
    pages = rest[:pps]
    sel_ref, kbuf, keys_ref, gmax_ref = rest[pps:]
    c = pl.program_id(1)
    n_steps = pl.num_programs(1)
    total = past_len + LANES
    span = pps * PAGE_SIZE
    q = qidx_ref[...].reshape(H_IDX * t_new, D_IDX).astype(BF16)

    def weighted(r):
        score = jnp.zeros((t_new, r.shape[1]), F32)
        for h in range(H_IDX):
            score = score + jnp.broadcast_to(widx_ref[:, h:h + 1], (t_new, r.shape[1])) * r[h * t_new:(h + 1) * t_new, :]
        return score

    @pl.when(c == 0)
    def _():
        gmax_ref[...] = jnp.full(gmax_ref.shape, -jnp.inf, F32)

    for p in range(pps):
        kbuf[:, p * PAGE_SIZE:(p + 1) * PAGE_SIZE] = pages[p][...].astype(BF16)
    off = pl.multiple_of(c * span, span)
    score = weighted(jnp.maximum(_dot(q, kbuf[...]), 0.0))
    keys_ref[:, pl.ds(off, span)] = _order_key(score)
    gmax = gmax_ref[...]
    for t in range(span // SCORE_TILE):
        gmax = jnp.maximum(gmax, score[:, t * SCORE_TILE:(t + 1) * SCORE_TILE])
    gmax_ref[...] = gmax

    @pl.when(c == n_steps - 1)
    def _():
        qi = lax.broadcasted_iota(I32, (t_new, LANES), 0)
        ki = lax.broadcasted_iota(I32, (t_new, LANES), 1)
        score_new = weighted(jnp.maximum(_dot_nt(q, knew_ref[...]), 0.0))
        keys_ref[:, past_len:total] = jnp.where(ki <= qi, _order_key(score_new), INT_MIN)
        lo0 = _order_key(jnp.min(gmax, axis=1, keepdims=True))
        top = jnp.maximum(jnp.max(gmax, axis=1, keepdims=True),
                          jnp.max(jnp.where(ki <= qi, score_new, -jnp.inf), axis=1, keepdims=True))
        thr = _select_threshold(keys_ref, t_new, None, total, n_keep, col_bits,
                                bracket=(lo0, _order_key(top) + 1))
        sel_ref[...] = jnp.where(keys_ref[...] >= thr, 1.0, 0.0)


def _samp_idx(page_table, qidx, widx, kidx_new_pad, cache, l, n_prompt, t_new):
    db, n_pages = page_table.shape
    pps = min(PAGES_PER_STEP, n_pages)
    past_len = n_pages * PAGE_SIZE
    total = past_len + LANES
    n_keep = min(TOPK_MAX, (past_len + t_new) // 4)
    assert (pps * PAGE_SIZE) % SCORE_TILE == 0 and n_keep <= SCORE_TILE
    base = n_prompt // t_new
    kern = functools.partial(_samp_idx_kernel, pps=pps, past_len=past_len, t_new=t_new, n_keep=n_keep,
                             col_bits=max(1, (total - 1).bit_length()))
    grid_spec = pltpu.PrefetchScalarGridSpec(
        num_scalar_prefetch=1,
        grid=(db, n_pages // pps),
        in_specs=[
            pl.BlockSpec((H_IDX, t_new, D_IDX), lambda b, c, pt: (0, b, 0)),
            pl.BlockSpec((t_new, H_IDX), lambda b, c, pt: (base + b, 0)),
            pl.BlockSpec((None, LANES, D_IDX), lambda b, c, pt: (b, 0, 0)),
        ] + _page_specs(l, (D_IDX, PAGE_SIZE), pps),
        out_specs=pl.BlockSpec((None, t_new, total), lambda b, c, pt: (b, 0, 0)),
        scratch_shapes=[pltpu.VMEM((D_IDX, pps * PAGE_SIZE), BF16),
                        pltpu.VMEM((t_new, total), I32),
                        pltpu.VMEM((t_new, SCORE_TILE), F32)],
    )
    return pl.pallas_call(
        kern, grid_spec=grid_spec,
        out_shape=jax.ShapeDtypeStruct((db, t_new, total), F32),
        compiler_params=_cparams(2), name="sample_indexer",
    )(page_table, qidx, widx, kidx_new_pad, *([cache] * pps))


def _samp_dsa_kernel(pt_ref, qa_ref, sel_ref, kvnew_ref, biasp_ref, biasn_ref, *rest, pps, past_len, t_new):
    pages = rest[:pps]
    out_ref, kbuf, m_ref, l_ref, acc_ref = rest[pps:]
    c = pl.program_id(1)
    n_steps = pl.num_programs(1)
    span = pps * PAGE_SIZE

    @pl.when(c == 0)
    def _():
        _softmax_init(m_ref, l_ref, acc_ref)

    def mask_add(sel):
        return jnp.concatenate([jnp.where(sel > 0.5, 0.0, NEG_BIG)] * H_A, axis=0)

    qa = qa_ref[...]
    q = jnp.concatenate([qa[:, h * DH_A:(h + 1) * DH_A] for h in range(H_A)], axis=0).astype(BF16)
    for p in range(pps):
        kbuf[p * PAGE_SIZE:(p + 1) * PAGE_SIZE, :] = pages[p][...].astype(BF16)
    off = pl.multiple_of(c * span, span)
    s = _dot_nt(q, kbuf[:, :DH_A]) + (biasp_ref[:, pl.ds(off, span)] + mask_add(sel_ref[:, pl.ds(off, span)]))
    _softmax_step(s, kbuf[:, DH_A:], m_ref, l_ref, acc_ref)

    @pl.when(c == n_steps - 1)
    def _():
        kvn = kvnew_ref[...]
        sn = _dot_nt(q, kvn[:, :DH_A]) + (biasn_ref[...] + mask_add(sel_ref[:, past_len:past_len + LANES]))
        _softmax_step(sn, kvn[:, DH_A:], m_ref, l_ref, acc_ref)
        o = acc_ref[...] / l_ref[...]
        for h in range(H_A):
            out_ref[:, h * DH_A:(h + 1) * DH_A] = o[h * t_new:(h + 1) * t_new, :]


def _samp_dsa(page_table, qa, sel, kva_new_pad, bias_past, bias_new, cache, l, t_new):
    db, n_pages = page_table.shape
    pps = min(PAGES_PER_STEP, n_pages)
    past_len = n_pages * PAGE_SIZE
    total = past_len + LANES
    rows = H_A * t_new
    kern = functools.partial(_samp_dsa_kernel, pps=pps, past_len=past_len, t_new=t_new)
    grid_spec = pltpu.PrefetchScalarGridSpec(
        num_scalar_prefetch=1,
        grid=(db, n_pages // pps),
        in_specs=[
            pl.BlockSpec((t_new, H_A * DH_A), lambda b, c, pt: (b, 0)),
            pl.BlockSpec((None, t_new, total), lambda b, c, pt: (b, 0, 0)),
            pl.BlockSpec((None, LANES, C_A), lambda b, c, pt: (b, 0, 0)),
            pl.BlockSpec((rows, past_len), lambda b, c, pt: (0, 0)),
            pl.BlockSpec((rows, LANES), lambda b, c, pt: (0, 0)),
        ] + _page_specs(l, (PAGE_SIZE, C_A), pps),
        out_specs=pl.BlockSpec((t_new, H_A * DH_A), lambda b, c, pt: (b, 0)),
        scratch_shapes=[
            pltpu.VMEM((pps * PAGE_SIZE, C_A), BF16),
            pltpu.VMEM((rows, LANES), F32),
            pltpu.VMEM((rows, LANES), F32),
            pltpu.VMEM((rows, DH_A), F32),
        ],
    )
    return pl.pallas_call(
        kern, grid_spec=grid_spec,
        out_shape=jax.ShapeDtypeStruct((db * t_new, H_A * DH_A), F32),
        compiler_params=_cparams(2), name="sample_dsa",
    )(page_table, qa, sel, kva_new_pad, bias_past, bias_new, *([cache] * pps))


def _samp_mla_kernel(pt_ref, q_ref, kvnew_ref, *rest, pps, t_new):
    pages = rest[:pps]
    out_ref, kbuf, m_ref, l_ref, acc_ref = rest[pps:]
    c = pl.program_id(1)
    n_steps = pl.num_programs(1)
    rows = H_B * t_new

    @pl.when(c == 0)
    def _():
        _softmax_init(m_ref, l_ref, acc_ref)

    q = q_ref[...].reshape(rows, C_B).astype(BF16)
    for p in range(pps):
        kbuf[:, p * PAGE_SIZE:(p + 1) * PAGE_SIZE] = pages[p][...].astype(BF16)
    _softmax_step(_dot(q, kbuf[...]), kbuf[0:KV_LORA, :], m_ref, l_ref, acc_ref, v_transposed=True)

    @pl.when(c == n_steps - 1)
    def _():
        kvn = kvnew_ref[...]
        qi = lax.broadcasted_iota(I32, (H_B, t_new, LANES), 1).reshape(rows, LANES)
        ki = lax.broadcasted_iota(I32, (rows, LANES), 1)
        sn = jnp.where(ki <= qi, _dot_nt(q, kvn), NEG_BIG)
        _softmax_step(sn, kvn[:, :KV_LORA], m_ref, l_ref, acc_ref)
        out_ref[...] = (acc_ref[...] / _lane_tile(l_ref[...], KV_LORA)).reshape(H_B, t_new, KV_LORA)


def _samp_mla(page_table, qcat, kvb_new_pad, cache, l, t_new):
    db, n_pages = page_table.shape
    pps = min(PAGES_PER_STEP, n_pages)
    rows = H_B * t_new
    kern = functools.partial(_samp_mla_kernel, pps=pps, t_new=t_new)
    grid_spec = pltpu.PrefetchScalarGridSpec(
        num_scalar_prefetch=1,
        grid=(db, n_pages // pps),
        in_specs=[
            pl.BlockSpec((H_B, t_new, C_B), lambda b, c, pt: (0, b, 0)),
            pl.BlockSpec((None, LANES, C_B), lambda b, c, pt: (b, 0, 0)),
        ] + _page_specs(l, (C_B, PAGE_SIZE), pps),
        out_specs=pl.BlockSpec((H_B, t_new, KV_LORA), lambda b, c, pt: (0, b, 0)),
        scratch_shapes=[
            pltpu.VMEM((C_B, pps * PAGE_SIZE), BF16),
            pltpu.VMEM((rows, LANES), F32),
            pltpu.VMEM((rows, LANES), F32),
            pltpu.VMEM((rows, KV_LORA), F32),
        ],
    )
    return pl.pallas_call(
        kern, grid_spec=grid_spec,
        out_shape=jax.ShapeDtypeStruct((H_B, db * t_new, KV_LORA), F32),
        compiler_params=_cparams(2), name="sample_mla",
    )(page_table, qcat, kvb_new_pad, *([cache] * pps))


def _merge_kernel(x_ref, a_ref, o_ref, wuv_ref, wout_ref, g_ref, b_ref, out_ref):
    width_a = H_A * DH_A
    mix = _dot(a_ref[...].astype(BF16), wout_ref[0:width_a, :])
    for h in range(H_B):
        b_h = _dot(o_ref[h].astype(BF16), wuv_ref[h]).astype(BF16)
        mix = mix + _dot(b_h, wout_ref[width_a + h * D_V:width_a + (h + 1) * D_V, :])
    out_ref[...] = _layer_norm(ALPHA * x_ref[...] + mix, g_ref[...], b_ref[...])


def _merge(x, a, o_lat, w, l):
    n = x.shape[0]
    tm = TOKEN_BLOCK
    row = lambda width: pl.BlockSpec((tm, width), lambda i: (i, 0))
    lay = lambda arr: pl.BlockSpec((None,) + arr.shape[1:], lambda i: (l,) + (0,) * (arr.ndim - 1))
    return pl.pallas_call(
        _merge_kernel,
        grid=(n // tm,),
        in_specs=[row(D_MODEL), row(H_A * DH_A), pl.BlockSpec((H_B, tm, KV_LORA), lambda i: (0, i, 0)),
                  lay(w["wuv"]), lay(w["wout"]), lay(w["ln1g"]), lay(w["ln1b"])],
        out_specs=row(D_MODEL),
        out_shape=jax.ShapeDtypeStruct((n, D_MODEL), F32),
        compiler_params=_cparams(1), name="merge_heads",
    )(x, a, o_lat, w["wuv"], w["wout"], w["ln1g"], w["ln1b"])


def _ffn_kernel(x_ref, wg_ref, wu_ref, wd_ref, g_ref, b_ref, out_ref, acc_ref):
    f = pl.program_id(1)

    @pl.when(f == 0)
    def _():
        acc_ref[...] = jnp.zeros(acc_ref.shape, F32)

    x = x_ref[...].astype(BF16)
    hid = jax.nn.silu(_dot(x, wg_ref[...])) * _dot(x, wu_ref[...])
    acc_ref[...] += _dot(hid.astype(BF16), wd_ref[...])

    @pl.when(f == pl.num_programs(1) - 1)
    def _():
        out_ref[...] = _layer_norm(ALPHA * x_ref[...] + acc_ref[...], g_ref[...], b_ref[...])


def _ff_chunk(d_ff):
    for parts in (2, 1, 11, 22):
        if d_ff % parts == 0 and (d_ff // parts) % LANES == 0:
            return d_ff // parts
    return d_ff


def _ffn_dense(x, w, i_dense, l):
    n = x.shape[0]
    tm = TOKEN_BLOCK
    d_ff = w["wg_d"].shape[-1]
    tf = _ff_chunk(d_ff)
    return pl.pallas_call(
        _ffn_kernel,
        grid=(n // tm, d_ff // tf),
        in_specs=[pl.BlockSpec((tm, D_MODEL), lambda i, f: (i, 0)),
                  pl.BlockSpec((None, D_MODEL, tf), lambda i, f: (i_dense, 0, f)),
                  pl.BlockSpec((None, D_MODEL, tf), lambda i, f: (i_dense, 0, f)),
                  pl.BlockSpec((None, tf, D_MODEL), lambda i, f: (i_dense, f, 0)),
                  pl.BlockSpec((None, 1, D_MODEL), lambda i, f: (l, 0, 0)),
                  pl.BlockSpec((None, 1, D_MODEL), lambda i, f: (l, 0, 0))],
        out_specs=pl.BlockSpec((tm, D_MODEL), lambda i, f: (i, 0)),
        out_shape=jax.ShapeDtypeStruct((n, D_MODEL), F32),
        scratch_shapes=[pltpu.VMEM((tm, D_MODEL), F32)],
        compiler_params=_cparams(2), name="ffn_dense",
    )(x, w["wg_d"], w["wu_d"], w["wd_d"], w["ln2g"], w["ln2b"])


def _router_kernel(x_ref, wr_ref, br_ref, gates_ref):
    logits = jnp.dot(x_ref[...], wr_ref[...], preferred_element_type=F32,
                     precision=lax.Precision.HIGHEST) + br_ref[...]
    lane = lax.broadcasted_iota(I32, logits.shape, 1).astype(F32)
    logits = jnp.where(lane < N_EXPERTS, logits, -jnp.inf)
    m1 = jnp.max(logits, axis=1, keepdims=True)
    i1 = jnp.min(jnp.where(logits == m1, lane, float(LANES)), axis=1, keepdims=True)
    rest = jnp.where(lane == i1, -jnp.inf, logits)
    m2 = jnp.max(rest, axis=1, keepdims=True)
    i2 = jnp.min(jnp.where(rest == m2, lane, float(LANES)), axis=1, keepdims=True)
    e2 = jnp.exp(m2 - m1)
    g1 = 1.0 / (1.0 + e2)
    g2 = e2 / (1.0 + e2)
    for e in range(N_EXPERTS):
        gate = jnp.where(i1 == e, g1, 0.0) + jnp.where(i2 == e, g2, 0.0)
        gates_ref[e] = jnp.broadcast_to(gate, (gate.shape[0], LANES))


def _router(x, w, i_moe):
    n = x.shape[0]
    tm = TOKEN_BLOCK
    return pl.pallas_call(
        _router_kernel,
        grid=(n // tm,),
        in_specs=[pl.BlockSpec((tm, D_MODEL), lambda i: (i, 0)),
                  pl.BlockSpec((None, D_MODEL, LANES), lambda i: (i_moe, 0, 0)),
                  pl.BlockSpec((None, 1, LANES), lambda i: (i_moe, 0, 0))],
        out_specs=pl.BlockSpec((N_EXPERTS, tm, LANES), lambda i: (0, i, 0)),
        out_shape=jax.ShapeDtypeStruct((N_EXPERTS, n, LANES), F32),
        compiler_params=_cparams(1), name="moe_router",
    )(x, w["wr"], w["br"])


def _moe_kernel(x_ref, gate_ref, wg_ref, wu_ref, wd_ref, g_ref, b_ref, out_ref, acc_ref):
    e = pl.program_id(1)
    f = pl.program_id(2)

    @pl.when((e == 0) & (f == 0))
    def _():
        acc_ref[...] = jnp.zeros(acc_ref.shape, F32)

    x = x_ref[...].astype(BF16)
    hid = jax.nn.silu(_dot(x, wg_ref[...])) * _dot(x, wu_ref[...])
    y = _dot(hid.astype(BF16), wd_ref[...])
    gate = jnp.concatenate([gate_ref[...]] * (D_MODEL // LANES), axis=1)
    acc_ref[...] += gate * y

    @pl.when((e == pl.num_programs(1) - 1) & (f == pl.num_programs(2) - 1))
    def _():
        out_ref[...] = _layer_norm(ALPHA * x_ref[...] + acc_ref[...], g_ref[...], b_ref[...])


def _ffn_moe(x, gates, w, i_moe, l):
    n = x.shape[0]
    tm = TOKEN_BLOCK
    d_ff = w["wg_e"].shape[-1]
    tf = _ff_chunk(d_ff)
    return pl.pallas_call(
        _moe_kernel,
        grid=(n // tm, N_EXPERTS, d_ff // tf),
        in_specs=[pl.BlockSpec((tm, D_MODEL), lambda i, e, f: (i, 0)),
                  pl.BlockSpec((None, tm, LANES), lambda i, e, f: (e, i, 0)),
                  pl.BlockSpec((None, None, D_MODEL, tf), lambda i, e, f: (i_moe, e, 0, f)),
                  pl.BlockSpec((None, None, D_MODEL, tf), lambda i, e, f: (i_moe, e, 0, f)),
                  pl.BlockSpec((None, None, tf, D_MODEL), lambda i, e, f: (i_moe, e, f, 0)),
                  pl.BlockSpec((None, 1, D_MODEL), lambda i, e, f: (l, 0, 0)),
                  pl.BlockSpec((None, 1, D_MODEL), lambda i, e, f: (l, 0, 0))],
        out_specs=pl.BlockSpec((tm, D_MODEL), lambda i, e, f: (i, 0)),
        out_shape=jax.ShapeDtypeStruct((n, D_MODEL), F32),
        scratch_shapes=[pltpu.VMEM((tm, D_MODEL), F32)],
        compiler_params=_cparams(3), name="ffn_moe",
    )(x, gates, w["wg_e"], w["wu_e"], w["wd_e"], w["ln2g"], w["ln2b"])


def _swap_halves(w):
    half = w.shape[-1] // 2
    return jnp.concatenate([w[..., half:], w[..., :half]], axis=-1)


def _prepare_weights(w_in, idx_ln_g, idx_ln_b, q_norm_g, w_uq, kv_norm_g, w_uk, w_uv, w_out, ln1_g, ln1_b,
                     ln2_g, ln2_b, w_gate_dense, w_up_dense, w_down_dense, w_router, b_router,
                     w_gate_exp, w_up_exp, w_down_exp):
    bf = lambda a: a.astype(BF16)
    depth = w_in.shape[0]
    splits = (H_A * DH_A, DH_A, DH_A, H_IDX * D_IDX, D_IDX, H_IDX, Q_LORA, KV_LORA, D_ROPE)
    cols, start = [], 0
    for width in splits:
        cols.append(w_in[:, :, start:start + width])
        start += width
    w_qa, w_ka, w_va, w_qidx, w_kidx, w_widx, w_cq, w_ckv, w_kr = cols
    w_uq4 = w_uq.reshape(depth, Q_LORA, H_B, D_NOPE + D_ROPE)
    w_uq_rope = w_uq4[..., D_NOPE:]
    vec = lambda a: a[:, None, :]
    return {
        "wqa": bf(w_qa),
        "wkva": bf(jnp.concatenate([w_ka, w_va], axis=-1)),
        "wqidx": bf(w_qidx.reshape(depth, D_MODEL, H_IDX, D_IDX).transpose(0, 2, 1, 3)),
        "wkidx": bf(w_kidx),
        "wwidx": bf(jnp.pad(w_widx, ((0, 0), (0, 0), (0, LANES - H_IDX)))),
        "wcq": bf(w_cq),
        "wckv": bf(w_ckv),
        "wkr": bf(jnp.stack([w_kr, _swap_halves(w_kr)], axis=1)),
        "idxg": vec(idx_ln_g), "idxb": vec(idx_ln_b), "qng": vec(q_norm_g), "kvng": vec(kv_norm_g),
        "wuqn": bf(w_uq4[..., :D_NOPE].transpose(0, 2, 1, 3)),
        "wuqr": bf(w_uq_rope.transpose(0, 2, 1, 3)),
        "wuqrs": bf(_swap_halves(w_uq_rope).transpose(0, 2, 1, 3)),
        "wukt": bf(w_uk.transpose(0, 2, 3, 1)),
        "wuv": bf(w_uv.transpose(0, 2, 1, 3)),
        "wout": bf(w_out),
        "ln1g": vec(ln1_g), "ln1b": vec(ln1_b), "ln2g": vec(ln2_g), "ln2b": vec(ln2_b),
        "wg_d": bf(w_gate_dense), "wu_d": bf(w_up_dense), "wd_d": bf(w_down_dense),
        "wr": jnp.pad(w_router, ((0, 0), (0, 0), (0, LANES - N_EXPERTS))),
        "br": jnp.pad(b_router, ((0, 0), (0, LANES - N_EXPERTS)))[:, None, :],
        "wg_e": bf(w_gate_exp), "wu_e": bf(w_up_exp), "wd_e": bf(w_down_exp),
    }


def _rope_tables(pos):
    half = D_ROPE // 2
    inv = ROPE_THETA ** (-jnp.arange(half, dtype=F32) / half)
    ang = pos.astype(F32)[:, None] * inv[None, :]
    cos, sin = jnp.cos(ang), jnp.sin(ang)
    return jnp.concatenate([cos, cos], axis=-1), jnp.concatenate([-sin, sin], axis=-1)


def _pad_new(a, db, t_new):
    a = a.reshape(db, t_new, a.shape[-1])
    return jnp.pad(a, ((0, 0), (0, LANES - t_new), (0, 0)))


def kernel(x_prompt, x_sample, cache_a_kv, cache_a_kidx, cache_b_latent, page_table, rel_bias, w_in, idx_ln_g, idx_ln_b, q_norm_g, w_uq, kv_norm_g, w_uk, w_uv, w_out, ln1_g, ln1_b, ln2_g, ln2_b, w_gate_dense, w_up_dense, w_down_dense, w_router, b_router, w_gate_exp, w_up_exp, w_down_exp):
    batch, seq, _ = x_prompt.shape
    db, t_new, _ = x_sample.shape
    depth = w_in.shape[0]
    n_prompt = batch * seq
    n_sample = db * t_new
    n = n_prompt + n_sample
    past_len = page_table.shape[1] * PAGE_SIZE
    assert seq % KEY_CHUNK == 0 and seq % MLA_KEY_CHUNK == 0 and n % TOKEN_BLOCK == 0 and n_prompt % t_new == 0 and t_new % 8 == 0
    assert page_table.shape[1] % min(PAGES_PER_STEP, page_table.shape[1]) == 0

    w = _prepare_weights(w_in, idx_ln_g, idx_ln_b, q_norm_g, w_uq, kv_norm_g, w_uk, w_uv, w_out, ln1_g, ln1_b,
                         ln2_g, ln2_b, w_gate_dense, w_up_dense, w_down_dense, w_router, b_router,
                         w_gate_exp, w_up_exp, w_down_exp)
    pos = jnp.concatenate([jnp.tile(jnp.arange(seq, dtype=I32), batch),
                           jnp.tile(past_len + jnp.arange(t_new, dtype=I32), db)])
    cos_t, sin_t = _rope_tables(pos)
    bias_tile, bias_past, bias_new = _bias_tables(rel_bias, past_len, t_new)

    cache_kidx_t = jnp.swapaxes(cache_a_kidx, 2, 3)
    cache_lat_t = jnp.swapaxes(cache_b_latent, 2, 3)

    x = jnp.concatenate([x_prompt.reshape(n_prompt, D_MODEL), x_sample.reshape(n_sample, D_MODEL)], axis=0)
    kva_all, kidx_all, kvb_all = [], [], []
    for l in range(depth):
        qa, kva, kva_bf, qidx, kidx, kidx_bf, widx, qcat, kvb, kvb_bf = _projections(x, cos_t, sin_t, w, l)
        a_p = _dsa_prompt(qa, qidx, widx, kidx_bf, kva_bf, bias_tile, batch, seq)
        o_p = _mla_prompt(qcat, kvb_bf, batch, seq)
        qidx_s = qidx[:, n_prompt:].astype(F32)
        qa_s = qa[n_prompt:].astype(F32)
        qcat_s = qcat[:, n_prompt:].astype(F32)
        sel = _samp_idx(page_table, qidx_s, widx, _pad_new(kidx_bf[n_prompt:], db, t_new), cache_kidx_t, l,
                        n_prompt, t_new)
        a_s = _samp_dsa(page_table, qa_s, sel, _pad_new(kva_bf[n_prompt:], db, t_new), bias_past, bias_new,
                        cache_a_kv, l, t_new)
        o_s = _samp_mla(page_table, qcat_s, _pad_new(kvb_bf[n_prompt:], db, t_new), cache_lat_t, l, t_new)
        a = jnp.concatenate([a_p, a_s], axis=0)
        o_lat = jnp.concatenate([o_p, o_s], axis=1)
        x = _merge(x, a, o_lat, w, l)
        if l % 2 == 0:
            x = _ffn_dense(x, w, l // 2, l)
        else:
            x = _ffn_moe(x, _router(x, w, l // 2), w, l // 2, l)
        kva_all.append(kva)
        kidx_all.append(kidx)
        kvb_all.append(kvb)

    def split(parts, width):
        full = jnp.stack(parts)
        return (full[:, :n_prompt].reshape(depth, batch, seq, width),
                full[:, n_prompt:].reshape(depth, db, t_new, width))

    kva_p, kva_s = split(kva_all, C_A)
    kidx_p, kidx_s = split(kidx_all, D_IDX)
    kvb_p, kvb_s = split(kvb_all, C_B)
    return (x[:n_prompt].reshape(batch, seq, D_MODEL), x[n_prompt:].reshape(db, t_new, D_MODEL),
            kva_p, kidx_p, kvb_p, kva_s, kidx_s, kvb_s)
```
